```python
import jax, jax.numpy as jnp
from jax import lax
import numpy as np

D_MODEL = 2048
BATCH = 2
SEQ = 8192
DEPTH = 1

CHUNK = 64
D_MIX = D_MODEL

HG_WIDTH = D_MIX // 2
HG_EXPAND = 128
HG_HEADS = HG_WIDTH // HG_EXPAND
HG_DK = HG_EXPAND
HG_DV = HG_WIDTH // HG_HEADS

MLA_WIDTH = D_MIX - HG_WIDTH
V_DIM = 128
MLA_HEADS = MLA_WIDTH // V_DIM
QK_NOPE = 128
QK_ROPE = 64
QK_DIM = QK_NOPE + QK_ROPE
Q_LORA = 512
KV_LORA = 512
ROPE_THETA = 10000.0
Q_BLOCK = 128
ATTN_SCALE = QK_DIM ** -0.5

D_FF = ((8 * D_MODEL // 3 + 255) // 256) * 256

NORM_EPS = 1e-6

IN_SIZES = (HG_WIDTH, HG_WIDTH, HG_WIDTH, HG_WIDTH, Q_LORA, KV_LORA, QK_ROPE)
D_IN = sum(IN_SIZES)
IN_SPLITS = tuple(int(v) for v in np.cumsum(IN_SIZES)[:-1])

kernel_name = "hymba_style_hgrn2_mla_swiglu"


def rms_norm(x, w):
    xf = x.astype(jnp.float32)
    y = xf * lax.rsqrt(jnp.mean(xf * xf, axis=-1, keepdims=True) + NORM_EPS)
    return (y * w.astype(jnp.float32)).astype(x.dtype)


def rope_tables(positions):
    inv_freq = 1.0 / (ROPE_THETA ** (jnp.arange(0, QK_ROPE, 2, dtype=jnp.float32) / QK_ROPE))
    ang = positions.astype(jnp.float32)[..., None] * inv_freq
    return jnp.cos(ang)[:, :, None, :], jnp.sin(ang)[:, :, None, :]


def rope_tail(t, cos, sin):
    t_nope, t_rope = jnp.split(t, [QK_NOPE], axis=-1)
    r1, r2 = jnp.split(t_rope, 2, axis=-1)
    c = cos.astype(t.dtype)
    s = sin.astype(t.dtype)
    return jnp.concatenate([t_nope, r1 * c - r2 * s, r1 * s + r2 * c], axis=-1)


def hgrn2_mixer(q, f_pre, i, g, lb, norm_w):
    B_, S_, _ = q.shape
    n_chunks = S_ // CHUNK
    qf = jax.nn.silu(q.astype(jnp.float32))
    f = lb + (1.0 - lb) * jax.nn.sigmoid(f_pre.astype(jnp.float32))
    kf = 1.0 - f

    def chunked(t, d):
        return t.reshape(B_, n_chunks, CHUNK, HG_HEADS, d).transpose(1, 0, 3, 2, 4)

    qc = chunked(qf, HG_DK)
    kc = chunked(kf, HG_DK)
    vc = chunked(i.astype(jnp.float32), HG_DV)
    bc = jnp.cumsum(chunked(jnp.log(f), HG_DK), axis=3)
    causal = jnp.tril(jnp.ones((CHUNK, CHUNK), dtype=bool))[:, :, None]

    def step(state, inp):
        q_c, k_c, v_c, b_c = inp
        rel = b_c[:, :, :, None, :] - b_c[:, :, None, :, :]
        decay = jnp.exp(jnp.where(causal, rel, -jnp.inf))
        scores = jnp.einsum('bhtd,bhsd,bhtsd->bhts', q_c, k_c, decay)
        o = (jnp.einsum('bhts,bhsv->bhtv', scores, v_c)
             + jnp.einsum('bhtd,bhdv->bhtv', q_c * jnp.exp(b_c), state))
        b_last = b_c[:, :, -1:, :]
        state = (jnp.exp(b_last[:, :, 0, :, None]) * state
                 + jnp.einsum('bhsd,bhsv->bhdv', k_c * jnp.exp(b_last - b_c), v_c))
        return state, o

    state0 = jnp.zeros((B_, HG_HEADS, HG_DK, HG_DV), jnp.float32)
    _, o = lax.scan(step, state0, (qc, kc, vc, bc))
    o = o.transpose(1, 0, 3, 2, 4).reshape(B_, S_, HG_HEADS, HG_DV)
    gate = jax.nn.silu(g.astype(jnp.float32)).reshape(B_, S_, HG_HEADS, HG_DV)
    o = rms_norm(o, norm_w) * gate
    return o.reshape(B_, S_, HG_WIDTH).astype(g.dtype)


def mla_mixer(c_q, c_kv, k_rope, cos, sin, q_norm_w, w_uq, kv_norm_w, w_ukv, q_head_w, k_head_w):
    B_, S_, _ = c_q.shape
    q = (rms_norm(c_q, q_norm_w) @ w_uq).reshape(B_, S_, MLA_HEADS, QK_DIM)
    kv = (rms_norm(c_kv, kv_norm_w) @ w_ukv).reshape(B_, S_, MLA_HEADS, QK_NOPE + V_DIM)
    k_nope, v = jnp.split(kv, [QK_NOPE], axis=-1)
    k = jnp.concatenate(
        [k_nope, jnp.broadcast_to(k_rope[:, :, None, :], (B_, S_, MLA_HEADS, QK_ROPE))], axis=-1)
    q = rope_tail(rms_norm(q, q_head_w), cos, sin)
    k = rope_tail(rms_norm(k, k_head_w), cos, sin)

    n_blocks = S_ // Q_BLOCK
    q_blocks = q.reshape(B_, n_blocks, Q_BLOCK, MLA_HEADS, QK_DIM).transpose(1, 0, 2, 3, 4)
    key_chunk = jnp.arange(S_) // CHUNK

    def attend(args):
        q_blk, blk = args
        query_chunk = (blk * Q_BLOCK + jnp.arange(Q_BLOCK)) // CHUNK
        s = jnp.einsum('bqhd,bkhd->bhqk', q_blk, k).astype(jnp.float32) * ATTN_SCALE
        s = jnp.where(key_chunk[None, :] <= query_chunk[:, None], s, -jnp.inf)
        p = jax.nn.softmax(s, axis=-1).astype(v.dtype)
        return jnp.einsum('bhqk,bkhv->bqhv', p, v)

    o = lax.map(attend, (q_blocks, jnp.arange(n_blocks)))
    return o.transpose(1, 0, 2, 3, 4).reshape(B_, S_, MLA_WIDTH)


def setup_inputs(seed: int = 0) -> dict:
    key = jax.random.key(seed)
    ks = jax.random.split(key, 17)
    f32 = jnp.float32

    def nrm(k, shape, fan_in):
        return jax.random.normal(k, shape, f32) * (fan_in ** -0.5)

    def gain(k, shape):
        return 1.0 + 0.02 * jax.random.normal(k, shape, f32)

    x = jax.random.normal(ks[0], (BATCH, SEQ, D_MODEL), f32)
    offset = jax.random.randint(ks[1], (BATCH, 1), 0, 4096, dtype=jnp.int32)
    positions = offset + jnp.arange(SEQ, dtype=jnp.int32)[None, :]
    return {
        "x": x,
        "positions": positions,
        "attn_norm_w": gain(ks[2], (DEPTH, D_MODEL)),
        "w_in": nrm(ks[3], (DEPTH, D_MODEL, D_IN), D_MODEL),
        "hgrn_lb": 0.5 * jax.random.normal(ks[4], (DEPTH + 1, HG_HEADS * HG_DK), f32),
        "hgrn_norm_w": gain(ks[5], (DEPTH, HG_DV)),
        "mla_q_norm_w": gain(ks[6], (DEPTH, Q_LORA)),
        "w_uq": nrm(ks[7], (DEPTH, Q_LORA, MLA_HEADS * QK_DIM), Q_LORA),
        "mla_kv_norm_w": gain(ks[8], (DEPTH, KV_LORA)),
        "w_ukv": nrm(ks[9], (DEPTH, KV_LORA, MLA_HEADS * (QK_NOPE + V_DIM)), KV_LORA),
        "q_head_norm_w": gain(ks[10], (DEPTH, QK_DIM)),
        "k_head_norm_w": gain(ks[11], (DEPTH, QK_DIM)),
        "w_out": nrm(ks[12], (DEPTH, D_MIX, D_MODEL), D_MIX),
        "ffn_norm_w": gain(ks[13], (DEPTH, D_MODEL)),
        "w_gate_up": nrm(ks[14], (DEPTH, D_MODEL, 2 * D_FF), D_MODEL),
        "w_down": nrm(ks[15], (DEPTH, D_FF, D_MODEL), D_FF),
    }


def reference(x, positions, attn_norm_w, w_in, hgrn_lb, hgrn_norm_w, mla_q_norm_w, w_uq,
              mla_kv_norm_w, w_ukv, q_head_norm_w, k_head_norm_w, w_out, ffn_norm_w,
              w_gate_up, w_down):
    cos, sin = rope_tables(positions)
    lower_bounds = jnp.cumsum(jax.nn.softmax(hgrn_lb.astype(jnp.float32), axis=0), axis=0)
    for layer in range(DEPTH):
        h = rms_norm(x, attn_norm_w[layer])
        proj = h @ w_in[layer]
        hq, hf, hi, hg, c_q, c_kv, k_r = jnp.split(proj, IN_SPLITS, axis=-1)
        o_a = hgrn2_mixer(hq, hf, hi, hg, lower_bounds[layer], hgrn_norm_w[layer])
        o_b = mla_mixer(c_q, c_kv, k_r, cos, sin, mla_q_norm_w[layer], w_uq[layer],
                        mla_kv_norm_w[layer], w_ukv[layer], q_head_norm_w[layer],
                        k_head_norm_w[layer])
        x = x + jnp.concatenate([o_a, o_b], axis=-1) @ w_out[layer]
        h = rms_norm(x, ffn_norm_w[layer])
        gate, up = jnp.split(h @ w_gate_up[layer], 2, axis=-1)
        x = x + (jax.nn.silu(gate) * up) @ w_down[layer]
    return x
```

```python
import functools

import jax
import jax.numpy as jnp
import numpy as np
from jax import lax
from jax.experimental import pallas as pl
from jax.experimental.pallas import tpu as pltpu

D_MODEL = 2048
CHUNK = 64
SUB = 16
HG_WIDTH = 1024
HG_HEADS = 8
HG_D = 128
MLA_HEADS = 8
V_DIM = 128
QK_NOPE = 128
QK_ROPE = 64
QK_DIM = QK_NOPE + QK_ROPE
QK_PAD = 256
Q_LORA = 512
KV_LORA = 512
ROPE_THETA = 10000.0
ATTN_SCALE = QK_DIM ** -0.5
D_FF = 5632
NORM_EPS = 1e-6
LANE = 128

VMEM_LIMIT = 56 * 1024 * 1024

_NT = (((1,), (1,)), ((), ()))


def _silu(x):
    return x * (1.0 / (1.0 + jnp.exp(-x)))


def _rms_scale(x):
    return lax.rsqrt(jnp.mean(x * x, axis=-1, keepdims=True) + NORM_EPS)


def _norm_matmul_kernel(x_ref, nw_ref, w_ref, o_ref, h_ref):
    @pl.when(pl.program_id(1) == 0)
    def _():
        x = x_ref[...]
        h_ref[...] = (x * _rms_scale(x) * nw_ref[...]).astype(h_ref.dtype)

    o_ref[...] = jnp.dot(h_ref[...], w_ref[...],
                         preferred_element_type=jnp.float32).astype(o_ref.dtype)


def _norm_matmul(x, nw, w, *, tm, tn, out_dtype, name):
    m, k = x.shape
    n = w.shape[1]
    return pl.pallas_call(
        _norm_matmul_kernel,
        grid=(m // tm, n // tn),
        in_specs=[pl.BlockSpec((tm, k), lambda i, j: (i, 0)),
                  pl.BlockSpec((1, k), lambda i, j: (0, 0)),
                  pl.BlockSpec((k, tn), lambda i, j: (0, j))],
        out_specs=pl.BlockSpec((tm, tn), lambda i, j: (i, j)),
        out_shape=jax.ShapeDtypeStruct((m, n), out_dtype),
        scratch_shapes=[pltpu.VMEM((tm, k), jnp.bfloat16)],
        compiler_params=pltpu.CompilerParams(
            dimension_semantics=("parallel", "arbitrary"),
            vmem_limit_bytes=VMEM_LIMIT),
        name=name,
    )(x, nw, w)


def _norm_swiglu_kernel(x_ref, nw_ref, wg_ref, wu_ref, o_ref, h_ref):
    @pl.when(pl.program_id(1) == 0)
    def _():
        x = x_ref[...]
        h_ref[...] = (x * _rms_scale(x) * nw_ref[...]).astype(h_ref.dtype)

    h = h_ref[...]
    g = jnp.dot(h, wg_ref[...], preferred_element_type=jnp.float32)
    u = jnp.dot(h, wu_ref[...], preferred_element_type=jnp.float32)
    o_ref[...] = (_silu(g) * u).astype(o_ref.dtype)


def _norm_swiglu(x, nw, w_gate_up, *, tm, tn):
    m, k = x.shape
    n = w_gate_up.shape[1] // 2
    up_off = n // tn
    return pl.pallas_call(
        _norm_swiglu_kernel,
        grid=(m // tm, n // tn),
        in_specs=[pl.BlockSpec((tm, k), lambda i, j: (i, 0)),
                  pl.BlockSpec((1, k), lambda i, j: (0, 0)),
                  pl.BlockSpec((k, tn), lambda i, j: (0, j)),
                  pl.BlockSpec((k, tn), lambda i, j: (0, j + up_off))],
        out_specs=pl.BlockSpec((tm, tn), lambda i, j: (i, j)),
        out_shape=jax.ShapeDtypeStruct((m, n), jnp.bfloat16),
        scratch_shapes=[pltpu.VMEM((tm, k), jnp.bfloat16)],
        compiler_params=pltpu.CompilerParams(
            dimension_semantics=("parallel", "arbitrary"),
            vmem_limit_bytes=VMEM_LIMIT),
        name="ffn_up",
    )(x, nw, w_gate_up, w_gate_up)


def _res_matmul_kernel(*refs, n_a):
    a_refs = refs[:n_a]
    w_refs = refs[n_a:2 * n_a]
    r_ref = refs[2 * n_a]
    o_ref = refs[2 * n_a + 1]
    acc = r_ref[...]
    for a_ref, w_ref in zip(a_refs, w_refs):
        acc = acc + jnp.dot(a_ref[...], w_ref[...], preferred_element_type=jnp.float32)
    o_ref[...] = acc


def _res_matmul(a_list, w, res, *, tm, tn, name):
    m, n = res.shape
    n_a = len(a_list)
    ka = a_list[0].shape[1]
    in_specs = [pl.BlockSpec((tm, ka), lambda i, j: (i, 0)) for _ in a_list]
    in_specs += [pl.BlockSpec((ka, tn), functools.partial(lambda i, j, kb: (kb, j), kb=kb))
                 for kb in range(n_a)]
    in_specs += [pl.BlockSpec((tm, tn), lambda i, j: (i, j))]
    return pl.pallas_call(
        functools.partial(_res_matmul_kernel, n_a=n_a),
        grid=(m // tm, n // tn),
        in_specs=in_specs,
        out_specs=pl.BlockSpec((tm, tn), lambda i, j: (i, j)),
        out_shape=jax.ShapeDtypeStruct((m, n), jnp.float32),
        compiler_params=pltpu.CompilerParams(
            dimension_semantics=("parallel", "arbitrary"),
            vmem_limit_bytes=VMEM_LIMIT),
        name=name,
    )(*a_list, *([w] * n_a), res)


def _segmented_cumsum(x, row, seg):
    pos = row % seg
    shift = 1
    while shift < seg:
        x = x + jnp.where(pos >= shift, pltpu.roll(x, shift, axis=0), 0.0)
        shift *= 2
    return x


def _hgrn_kernel(q_ref, f_ref, i_ref, g_ref, lb_ref, nw_ref, o_ref,
                 state_ref, qf_ref, qt_ref, qs_ref, k_ref, b16_ref, b64_ref, oacc_ref,
                 *, rows):
    n_chunks = rows // CHUNK
    n_sub = CHUNK // SUB

    @pl.when(pl.program_id(2) == 0)
    def _():
        state_ref[...] = jnp.zeros_like(state_ref)

    lbr = lb_ref[...]
    lbe = jnp.exp(lbr - jnp.max(lbr, axis=0, keepdims=True))
    lb = lbe[0:1, :] / jnp.sum(lbe, axis=0, keepdims=True)

    qf = _silu(q_ref[...])
    f = lb + (1.0 - lb) * (1.0 / (1.0 + jnp.exp(-f_ref[...])))
    logf = jnp.log(f)
    row = lax.broadcasted_iota(jnp.int32, (rows, HG_D), 0)
    b16 = _segmented_cumsum(logf, row, SUB)
    b64 = _segmented_cumsum(logf, row, CHUNK)
    qf_ref[...] = qf
    qt_ref[...] = qf * jnp.exp(b16)
    qs_ref[...] = qf * jnp.exp(b64)
    k_ref[...] = 1.0 - f
    b16_ref[...] = b16
    b64_ref[...] = b64

    r_i = lax.broadcasted_iota(jnp.int32, (CHUNK, CHUNK), 0)
    c_i = lax.broadcasted_iota(jnp.int32, (CHUNK, CHUNK), 1)
    causal = c_i <= r_i
    off_diag = c_i < (r_i // SUB) * SUB
    lane8 = lax.broadcasted_iota(jnp.int32, (8, CHUNK), 1)

    def chunk_body(c, carry):
        r0 = pl.multiple_of(c * CHUNK, CHUNK)
        sl = pl.ds(r0, CHUNK)
        kc = k_ref[sl, :]
        vc = i_ref[sl, :]
        bc = b64_ref[sl, :]

        k_parts = []
        for blk in range(1, n_sub + 1):
            b_ref_pt = bc[blk * SUB - 1:blk * SUB, :]
            k_parts.append(kc * jnp.exp(jnp.minimum(b_ref_pt - bc, 0.0)))
        k_upd = k_parts[-1]
        k_stack = jnp.concatenate(k_parts[:-1], axis=0).astype(jnp.bfloat16)
        s_all = lax.dot_general(qt_ref[sl, :].astype(jnp.bfloat16), k_stack, _NT,
                                preferred_element_type=jnp.float32)
        off_rows = [jnp.zeros((SUB, CHUNK), jnp.float32)]
        for blk in range(1, n_sub):
            off_rows.append(s_all[blk * SUB:(blk + 1) * SUB, (blk - 1) * CHUNK:blk * CHUNK])
        s_off = jnp.concatenate(off_rows, axis=0)

        diag_rows = []
        for blk in range(n_sub):
            bs = pl.ds(pl.multiple_of(r0 + blk * SUB, SUB), SUB)
            q_b = qf_ref[bs, :]
            b_b = b16_ref[bs, :]
            k_b = k_ref[bs, :]
            for half in range(SUB // 8):
                q_h = q_b[half * 8:(half + 1) * 8, :]
                b_h = b_b[half * 8:(half + 1) * 8, :]
                tile = jnp.zeros((8, CHUNK), jnp.float32)
                for s in range((half + 1) * 8):
                    e = jnp.exp(jnp.minimum(b_h - b_b[s:s + 1, :], 0.0))
                    col = jnp.sum(q_h * k_b[s:s + 1, :] * e, axis=-1, keepdims=True)
                    tile = jnp.where(lane8 == blk * SUB + s, col, tile)
                diag_rows.append(tile)
        s_diag = jnp.concatenate(diag_rows, axis=0)

        scores = jnp.where(causal, jnp.where(off_diag, s_off, s_diag), 0.0)

        state_t = state_ref[...]
        o_c = jnp.dot(scores.astype(jnp.bfloat16), vc.astype(jnp.bfloat16),
                      preferred_element_type=jnp.float32)
        o_c = o_c + lax.dot_general(qs_ref[sl, :].astype(jnp.bfloat16),
                                    state_t.astype(jnp.bfloat16), _NT,
                                    preferred_element_type=jnp.float32)
        oacc_ref[sl, :] = o_c

        b_last = bc[CHUNK - 1:CHUNK, :]
        upd = jnp.dot(vc.T.astype(jnp.bfloat16), k_upd.astype(jnp.bfloat16),
                      preferred_element_type=jnp.float32)
        state_ref[...] = state_t * jnp.exp(b_last) + upd
        return carry

    lax.fori_loop(0, n_chunks, chunk_body, 0)

    o = oacc_ref[...]
    o = o * _rms_scale(o) * nw_ref[...]
    o_ref[...] = (o * _silu(g_ref[...])).astype(o_ref.dtype)


def _hgrn(proj_hg, hgrn_lb, norm_w, *, batch, seq, rows):
    n_row_blocks = seq // rows
    hb = HG_HEADS

    def col_spec(group):
        return pl.BlockSpec((rows, HG_D),
                            lambda b, h, t: (b * n_row_blocks + t, group * hb + h))

    scratch = [pltpu.VMEM((HG_D, HG_D), jnp.float32)]
    scratch += [pltpu.VMEM((rows, HG_D), jnp.float32) for _ in range(7)]
    return pl.pallas_call(
        functools.partial(_hgrn_kernel, rows=rows),
        grid=(batch, hb, n_row_blocks),
        in_specs=[col_spec(0), col_spec(1), col_spec(2), col_spec(3),
                  pl.BlockSpec((hgrn_lb.shape[0], HG_D), lambda b, h, t: (0, h)),
                  pl.BlockSpec((1, HG_D), lambda b, h, t: (0, 0))],
        out_specs=pl.BlockSpec((rows, HG_D), lambda b, h, t: (b * n_row_blocks + t, h)),
        out_shape=jax.ShapeDtypeStruct((batch * seq, HG_WIDTH), jnp.bfloat16),
        scratch_shapes=scratch,
        compiler_params=pltpu.CompilerParams(
            dimension_semantics=("parallel", "parallel", "arbitrary"),
            vmem_limit_bytes=VMEM_LIMIT),
        name="hgrn2",
    )(proj_hg, proj_hg, proj_hg, proj_hg, hgrn_lb, norm_w)


def _rope(u, cos_t, sin_t):
    return u * cos_t + pltpu.roll(u, LANE // 2, axis=1) * sin_t


def _mla_prep_kernel(pm_ref, pos_ref, qnw_ref, kvnw_ref, wuq_ref, wukv_ref,
                     qhw_ref, khw_ref, invf_ref, sgn_ref, q_ref, k_ref, v_ref):
    cq = pm_ref[:, 0:Q_LORA]
    ckv = pm_ref[:, Q_LORA:Q_LORA + KV_LORA]
    kr = pm_ref[:, Q_LORA + KV_LORA:Q_LORA + KV_LORA + LANE]
    cqn = (cq * _rms_scale(cq) * qnw_ref[...]).astype(jnp.bfloat16)
    ckvn = (ckv * _rms_scale(ckv) * kvnw_ref[...]).astype(jnp.bfloat16)
    q = jnp.dot(cqn, wuq_ref[...], preferred_element_type=jnp.float32)
    kv = jnp.dot(ckvn, wukv_ref[...], preferred_element_type=jnp.float32)

    ang = pos_ref[...].astype(jnp.float32) * invf_ref[...]
    cos_t = jnp.cos(ang)
    sin_t = jnp.sin(ang) * sgn_ref[...]

    qhw = qhw_ref[...]
    khw = khw_ref[...]
    kr_ss = jnp.sum(kr * kr, axis=-1, keepdims=True)
    for h in range(MLA_HEADS):
        c0 = h * QK_PAD
        qn = q[:, c0:c0 + LANE]
        qr = q[:, c0 + LANE:c0 + QK_PAD]
        ss = jnp.sum(qn * qn, axis=-1, keepdims=True) + jnp.sum(qr * qr, axis=-1, keepdims=True)
        inv = lax.rsqrt(ss * (1.0 / QK_DIM) + NORM_EPS) * ATTN_SCALE
        q_ref[:, c0:c0 + LANE] = (qn * inv * qhw[:, 0:LANE]).astype(q_ref.dtype)
        q_ref[:, c0 + LANE:c0 + QK_PAD] = _rope(qr * inv * qhw[:, LANE:QK_PAD],
                                                cos_t, sin_t).astype(q_ref.dtype)

        kn = kv[:, c0:c0 + LANE]
        ss = jnp.sum(kn * kn, axis=-1, keepdims=True) + kr_ss
        inv = lax.rsqrt(ss * (1.0 / QK_DIM) + NORM_EPS)
        k_ref[:, c0:c0 + LANE] = (kn * inv * khw[:, 0:LANE]).astype(k_ref.dtype)
        k_ref[:, c0 + LANE:c0 + QK_PAD] = _rope(kr * inv * khw[:, LANE:QK_PAD],
                                                cos_t, sin_t).astype(k_ref.dtype)
        v_ref[:, h * V_DIM:(h + 1) * V_DIM] = kv[:, c0 + LANE:c0 + QK_PAD].astype(v_ref.dtype)


def _mla_prep(proj_mla, pos, qnw, kvnw, wuq, wukv, qhw, khw, invf, sgn, *, tm):
    m = proj_mla.shape[0]
    full = lambda a: pl.BlockSpec(a.shape, lambda i: (0, 0))
    return pl.pallas_call(
        _mla_prep_kernel,
        grid=(m // tm,),
        in_specs=[pl.BlockSpec((tm, proj_mla.shape[1]), lambda i: (i, 0)),
                  pl.BlockSpec((tm, 1), lambda i: (i, 0)),
                  full(qnw), full(kvnw), full(wuq), full(wukv),
                  full(qhw), full(khw), full(invf), full(sgn)],
        out_specs=[pl.BlockSpec((tm, MLA_HEADS * QK_PAD), lambda i: (i, 0)),
                   pl.BlockSpec((tm, MLA_HEADS * QK_PAD), lambda i: (i, 0)),
                   pl.BlockSpec((tm, MLA_HEADS * V_DIM), lambda i: (i, 0))],
        out_shape=[jax.ShapeDtypeStruct((m, MLA_HEADS * QK_PAD), jnp.bfloat16),
                   jax.ShapeDtypeStruct((m, MLA_HEADS * QK_PAD), jnp.bfloat16),
                   jax.ShapeDtypeStruct((m, MLA_HEADS * V_DIM), jnp.bfloat16)],
        compiler_params=pltpu.CompilerParams(
            dimension_semantics=("parallel",),
            vmem_limit_bytes=VMEM_LIMIT),
        name="mla_prep",
    )(proj_mla, pos, qnw, kvnw, wuq, wukv, qhw, khw, invf, sgn)


def _attn_kernel(q_ref, k_ref, v_ref, o_ref, m_ref, l_ref, acc_ref, *, tile):
    i = pl.program_id(2)
    q = q_ref[...]
    m_ref[...] = jnp.full_like(m_ref, -jnp.inf)
    l_ref[...] = jnp.zeros_like(l_ref)
    acc_ref[...] = jnp.zeros_like(acc_ref)

    r_i = lax.broadcasted_iota(jnp.int32, (tile, tile), 0)
    c_i = lax.broadcasted_iota(jnp.int32, (tile, tile), 1)
    visible = (c_i // CHUNK) <= (r_i // CHUNK)

    def step(j, masked):
        ks = pl.ds(pl.multiple_of(j * tile, tile), tile)
        s = lax.dot_general(q, k_ref[ks, :], _NT, preferred_element_type=jnp.float32)
        if masked:
            s = jnp.where(visible, s, -jnp.inf)
        m_prev = m_ref[...]
        m_new = jnp.maximum(m_prev, jnp.max(s, axis=-1, keepdims=True))
        alpha = jnp.exp(m_prev - m_new)
        p = jnp.exp(s - m_new)
        l_ref[...] = alpha * l_ref[...] + jnp.sum(p, axis=-1, keepdims=True)
        acc_ref[...] = alpha * acc_ref[...] + jnp.dot(
            p.astype(v_ref.dtype), v_ref[ks, :], preferred_element_type=jnp.float32)
        m_ref[...] = m_new

    def body(j, carry):
        step(j, False)
        return carry

    lax.fori_loop(0, i, body, 0)
    step(i, True)
    o_ref[...] = (acc_ref[...] / l_ref[...]).astype(o_ref.dtype)


def _attention(q, k, v, *, batch, seq, tile):
    nq = seq // tile
    return pl.pallas_call(
        functools.partial(_attn_kernel, tile=tile),
        grid=(batch, MLA_HEADS, nq),
        in_specs=[pl.BlockSpec((tile, QK_PAD), lambda b, h, i: (b * nq + i, h)),
                  pl.BlockSpec((seq, QK_PAD), lambda b, h, i: (b, h)),
                  pl.BlockSpec((seq, V_DIM), lambda b, h, i: (b, h))],
        out_specs=pl.BlockSpec((tile, V_DIM), lambda b, h, i: (b * nq + i, h)),
        out_shape=jax.ShapeDtypeStruct((batch * seq, MLA_HEADS * V_DIM), jnp.bfloat16),
        scratch_shapes=[pltpu.VMEM((tile, 1), jnp.float32),
                        pltpu.VMEM((tile, 1), jnp.float32),
                        pltpu.VMEM((tile, V_DIM), jnp.float32)],
        compiler_params=pltpu.CompilerParams(
            dimension_semantics=("parallel", "parallel", "arbitrary"),
            vmem_limit_bytes=VMEM_LIMIT),
        name="mla_attention",
    )(q, k, v)


def _rope_lanes(a):
    half = QK_ROPE // 2
    z = jnp.zeros(a.shape[:-1] + (LANE // 2 - half,), a.dtype)
    return jnp.concatenate([a[..., :half], z, a[..., half:], z], axis=-1)


def _head_lanes(a):
    return jnp.concatenate([a[..., :QK_NOPE], _rope_lanes(a[..., QK_NOPE:])], axis=-1)


def kernel(x, positions, attn_norm_w, w_in, hgrn_lb, hgrn_norm_w, mla_q_norm_w, w_uq,
           mla_kv_norm_w, w_ukv, q_head_norm_w, k_head_norm_w, w_out, ffn_norm_w,
           w_gate_up, w_down):
    batch, seq, d = x.shape
    m = batch * seq
    bf = jnp.bfloat16
    layer = 0
    xf = x.reshape(m, d)

    hg_cols = 4 * HG_WIDTH
    w_in_l = w_in[layer]
    w_hg = w_in_l[:, :hg_cols].astype(bf)
    w_mla = jnp.concatenate(
        [w_in_l[:, hg_cols:hg_cols + Q_LORA + KV_LORA],
         _rope_lanes(w_in_l[:, hg_cols + Q_LORA + KV_LORA:])], axis=-1).astype(bf)
    wuq = _head_lanes(w_uq[layer].reshape(Q_LORA, MLA_HEADS, QK_DIM)).reshape(
        Q_LORA, MLA_HEADS * QK_PAD).astype(bf)
    wukv = w_ukv[layer].astype(bf)
    qhw = _head_lanes(q_head_norm_w[layer])[None, :]
    khw = _head_lanes(k_head_norm_w[layer])[None, :]
    inv_freq = 1.0 / (ROPE_THETA ** (jnp.arange(0, QK_ROPE, 2, dtype=jnp.float32) / QK_ROPE))
    invf = _rope_lanes(jnp.concatenate([inv_freq, inv_freq]))[None, :]
    half = QK_ROPE // 2
    sgn = _rope_lanes(jnp.concatenate([-jnp.ones((half,), jnp.float32),
                                       jnp.ones((half,), jnp.float32)]))[None, :]
    nw_attn = attn_norm_w[layer][None, :]

    proj_hg = _norm_matmul(xf, nw_attn, w_hg, tm=1024, tn=512,
                           out_dtype=jnp.float32, name="in_proj_hgrn")
    proj_mla = _norm_matmul(xf, nw_attn, w_mla, tm=1024, tn=w_mla.shape[1],
                            out_dtype=jnp.float32, name="in_proj_mla")

    o_a = _hgrn(proj_hg, hgrn_lb, hgrn_norm_w[layer][None, :],
                batch=batch, seq=seq, rows=512)

    q, k, v = _mla_prep(proj_mla, positions.reshape(m, 1),
                        mla_q_norm_w[layer][None, :], mla_kv_norm_w[layer][None, :],
                        wuq, wukv, qhw, khw, invf, sgn, tm=256)
    o_b = _attention(q, k, v, batch=batch, seq=seq, tile=512)

    x1 = _res_matmul([o_a, o_b], w_out[layer].astype(bf), xf, tm=1024, tn=512, name="out_proj")

    act = _norm_swiglu(x1, ffn_norm_w[layer][None, :], w_gate_up[layer].astype(bf),
                       tm=1024, tn=512)
    x2 = _res_matmul([act], w_down[layer].astype(bf), x1, tm=512, tn=512, name="ffn_down")
    return x2.reshape(batch, seq, d)
```

```python
import functools

import jax
import jax.numpy as jnp
import numpy as np
from jax import lax
from jax.experimental import pallas as pl
from jax.experimental.pallas import tpu as pltpu

D_MODEL = 2048
CHUNK = 64
SUB = 16
HG_WIDTH = 1024
HG_HEADS = 8
HG_D = 128
MLA_HEADS = 8
V_DIM = 128
QK_NOPE = 128
QK_ROPE = 64
QK_DIM = QK_NOPE + QK_ROPE
QK_PAD = 256
Q_LORA = 512
KV_LORA = 512
ROPE_THETA = 10000.0
ATTN_SCALE = QK_DIM ** -0.5
LOG2_E = 1.4426950408889634
D_FF = 5632
NORM_EPS = 1e-6
LANE = 128

VMEM_LIMIT = 56 * 1024 * 1024

_NT = (((1,), (1,)), ((), ()))


def _silu(x):
    return x * (1.0 / (1.0 + jnp.exp(-x)))


def _rms_scale(x):
    return lax.rsqrt(jnp.mean(x * x, axis=-1, keepdims=True) + NORM_EPS)


def _norm_matmul_kernel(x_ref, nw_ref, w_ref, o_ref, h_ref):
    @pl.when(pl.program_id(1) == 0)
    def _():
        x = x_ref[...]
        h_ref[...] = (x * _rms_scale(x) * nw_ref[...]).astype(h_ref.dtype)

    o_ref[...] = jnp.dot(h_ref[...], w_ref[...],
                         preferred_element_type=jnp.float32).astype(o_ref.dtype)


def _norm_matmul(x, nw, w, *, tm, tn, out_dtype, name):
    m, k = x.shape
    n = w.shape[1]
    return pl.pallas_call(
        _norm_matmul_kernel,
        grid=(m // tm, n // tn),
        in_specs=[pl.BlockSpec((tm, k), lambda i, j: (i, 0)),
                  pl.BlockSpec((1, k), lambda i, j: (0, 0)),
                  pl.BlockSpec((k, tn), lambda i, j: (0, j))],
        out_specs=pl.BlockSpec((tm, tn), lambda i, j: (i, j)),
        out_shape=jax.ShapeDtypeStruct((m, n), out_dtype),
        scratch_shapes=[pltpu.VMEM((tm, k), jnp.bfloat16)],
        compiler_params=pltpu.CompilerParams(
            dimension_semantics=("parallel", "arbitrary"),
            vmem_limit_bytes=VMEM_LIMIT),
        name=name,
    )(x, nw, w)


def _norm_swiglu_kernel(x_ref, nw_ref, wg_ref, wu_ref, o_ref, h_ref):
    @pl.when(pl.program_id(1) == 0)
    def _():
        x = x_ref[...]
        h_ref[...] = (x * _rms_scale(x) * nw_ref[...]).astype(h_ref.dtype)

    h = h_ref[...]
    g = jnp.dot(h, wg_ref[...], preferred_element_type=jnp.float32)
    u = jnp.dot(h, wu_ref[...], preferred_element_type=jnp.float32)
    o_ref[...] = (_silu(g) * u).astype(o_ref.dtype)


def _norm_swiglu(x, nw, w_gate_up, *, tm, tn):
    m, k = x.shape
    n = w_gate_up.shape[1] // 2
    up_off = n // tn
    return pl.pallas_call(
        _norm_swiglu_kernel,
        grid=(m // tm, n // tn),
        in_specs=[pl.BlockSpec((tm, k), lambda i, j: (i, 0)),
                  pl.BlockSpec((1, k), lambda i, j: (0, 0)),
                  pl.BlockSpec((k, tn), lambda i, j: (0, j)),
                  pl.BlockSpec((k, tn), lambda i, j: (0, j + up_off))],
        out_specs=pl.BlockSpec((tm, tn), lambda i, j: (i, j)),
        out_shape=jax.ShapeDtypeStruct((m, n), jnp.bfloat16),
        scratch_shapes=[pltpu.VMEM((tm, k), jnp.bfloat16)],
        compiler_params=pltpu.CompilerParams(
            dimension_semantics=("parallel", "arbitrary"),
            vmem_limit_bytes=VMEM_LIMIT),
        name="ffn_up",
    )(x, nw, w_gate_up, w_gate_up)


def _res_matmul_kernel(*refs, n_a):
    a_refs = refs[:n_a]
    w_refs = refs[n_a:2 * n_a]
    r_ref = refs[2 * n_a]
    o_ref = refs[2 * n_a + 1]
    acc = r_ref[...]
    for a_ref, w_ref in zip(a_refs, w_refs):
        acc = acc + jnp.dot(a_ref[...], w_ref[...], preferred_element_type=jnp.float32)
    o_ref[...] = acc


def _res_matmul(a_list, w, res, *, tm, tn, name):
    m, n = res.shape
    n_a = len(a_list)
    ka = a_list[0].shape[1]
    in_specs = [pl.BlockSpec((tm, ka), lambda i, j: (i, 0)) for _ in a_list]
    in_specs += [pl.BlockSpec((ka, tn), functools.partial(lambda i, j, kb: (kb, j), kb=kb))
                 for kb in range(n_a)]
    in_specs += [pl.BlockSpec((tm, tn), lambda i, j: (i, j))]
    return pl.pallas_call(
        functools.partial(_res_matmul_kernel, n_a=n_a),
        grid=(m // tm, n // tn),
        in_specs=in_specs,
        out_specs=pl.BlockSpec((tm, tn), lambda i, j: (i, j)),
        out_shape=jax.ShapeDtypeStruct((m, n), jnp.float32),
        compiler_params=pltpu.CompilerParams(
            dimension_semantics=("parallel", "arbitrary"),
            vmem_limit_bytes=VMEM_LIMIT),
        name=name,
    )(*a_list, *([w] * n_a), res)


def _segmented_cumsum(x, row, seg):
    pos = row % seg
    shift = 1
    while shift < seg:
        x = x + jnp.where(pos >= shift, pltpu.roll(x, shift, axis=0), 0.0)
        shift *= 2
    return x


def _hgrn_kernel(q_ref, f_ref, i_ref, g_ref, lb_ref, nw_ref, o_ref,
                 state_ref, qf_ref, qt_ref, qs_ref, k_ref, b16_ref, b64_ref, oacc_ref,
                 *, rows):
    n_chunks = rows // CHUNK
    n_sub = CHUNK // SUB

    @pl.when(pl.program_id(2) == 0)
    def _():
        state_ref[...] = jnp.zeros_like(state_ref)

    lbr = lb_ref[...]
    lbe = jnp.exp(lbr - jnp.max(lbr, axis=0, keepdims=True))
    lb = lbe[0:1, :] / jnp.sum(lbe, axis=0, keepdims=True)

    qf = _silu(q_ref[...])
    f = lb + (1.0 - lb) * (1.0 / (1.0 + jnp.exp(-f_ref[...])))
    logf = jnp.log(f)
    row = lax.broadcasted_iota(jnp.int32, (rows, HG_D), 0)
    b16 = _segmented_cumsum(logf, row, SUB)
    b64 = _segmented_cumsum(logf, row, CHUNK)
    qf_ref[...] = qf
    qt_ref[...] = qf * jnp.exp(b16)
    qs_ref[...] = qf * jnp.exp(b64)
    k_ref[...] = 1.0 - f
    b16_ref[...] = b16
    b64_ref[...] = b64

    r_i = lax.broadcasted_iota(jnp.int32, (CHUNK, CHUNK), 0)
    c_i = lax.broadcasted_iota(jnp.int32, (CHUNK, CHUNK), 1)
    causal = c_i <= r_i
    off_diag = c_i < (r_i // SUB) * SUB
    lane8 = lax.broadcasted_iota(jnp.int32, (8, CHUNK), 1)

    def chunk_body(c, carry):
        r0 = pl.multiple_of(c * CHUNK, CHUNK)
        sl = pl.ds(r0, CHUNK)
        kc = k_ref[sl, :]
        vc = i_ref[sl, :]
        bc = b64_ref[sl, :]

        k_parts = []
        for blk in range(1, n_sub + 1):
            b_ref_pt = bc[blk * SUB - 1:blk * SUB, :]
            k_parts.append(kc * jnp.exp(jnp.minimum(b_ref_pt - bc, 0.0)))
        k_upd = k_parts[-1]
        k_stack = jnp.concatenate(k_parts[:-1], axis=0).astype(jnp.bfloat16)
        s_all = lax.dot_general(qt_ref[sl, :].astype(jnp.bfloat16), k_stack, _NT,
                                preferred_element_type=jnp.float32)
        off_rows = [jnp.zeros((SUB, CHUNK), jnp.float32)]
        for blk in range(1, n_sub):
            off_rows.append(s_all[blk * SUB:(blk + 1) * SUB, (blk - 1) * CHUNK:blk * CHUNK])
        s_off = jnp.concatenate(off_rows, axis=0)

        diag_rows = []
        for blk in range(n_sub):
            bs = pl.ds(pl.multiple_of(r0 + blk * SUB, SUB), SUB)
            q_b = qf_ref[bs, :]
            b_b = b16_ref[bs, :]
            k_b = k_ref[bs, :]
            for half in range(SUB // 8):
                q_h = q_b[half * 8:(half + 1) * 8, :]
                b_h = b_b[half * 8:(half + 1) * 8, :]
                tile = jnp.zeros((8, CHUNK), jnp.float32)
                for s in range((half + 1) * 8):
                    e = jnp.exp(jnp.minimum(b_h - b_b[s:s + 1, :], 0.0))
                    col = jnp.sum(q_h * k_b[s:s + 1, :] * e, axis=-1, keepdims=True)
                    tile = jnp.where(lane8 == blk * SUB + s, col, tile)
                diag_rows.append(tile)
        s_diag = jnp.concatenate(diag_rows, axis=0)

        scores = jnp.where(causal, jnp.where(off_diag, s_off, s_diag), 0.0)

        state_t = state_ref[...]
        o_c = jnp.dot(scores.astype(jnp.bfloat16), vc.astype(jnp.bfloat16),
                      preferred_element_type=jnp.float32)
        o_c = o_c + lax.dot_general(qs_ref[sl, :].astype(jnp.bfloat16),
                                    state_t.astype(jnp.bfloat16), _NT,
                                    preferred_element_type=jnp.float32)
        oacc_ref[sl, :] = o_c

        b_last = bc[CHUNK - 1:CHUNK, :]
        upd = jnp.dot(vc.T.astype(jnp.bfloat16), k_upd.astype(jnp.bfloat16),
                      preferred_element_type=jnp.float32)
        state_ref[...] = state_t * jnp.exp(b_last) + upd
        return carry

    lax.fori_loop(0, n_chunks, chunk_body, 0)

    o = oacc_ref[...]
    o = o * _rms_scale(o) * nw_ref[...]
    o_ref[...] = (o * _silu(g_ref[...])).astype(o_ref.dtype)


def _hgrn(proj_hg, hgrn_lb, norm_w, *, batch, seq, rows):
    n_row_blocks = seq // rows
    hb = HG_HEADS

    def col_spec(group):
        return pl.BlockSpec((rows, HG_D),
                            lambda b, h, t: (b * n_row_blocks + t, group * hb + h))

    scratch = [pltpu.VMEM((HG_D, HG_D), jnp.float32)]
    scratch += [pltpu.VMEM((rows, HG_D), jnp.float32) for _ in range(7)]
    return pl.pallas_call(
        functools.partial(_hgrn_kernel, rows=rows),
        grid=(batch, hb, n_row_blocks),
        in_specs=[col_spec(0), col_spec(1), col_spec(2), col_spec(3),
                  pl.BlockSpec((hgrn_lb.shape[0], HG_D), lambda b, h, t: (0, h)),
                  pl.BlockSpec((1, HG_D), lambda b, h, t: (0, 0))],
        out_specs=pl.BlockSpec((rows, HG_D), lambda b, h, t: (b * n_row_blocks + t, h)),
        out_shape=jax.ShapeDtypeStruct((batch * seq, HG_WIDTH), jnp.bfloat16),
        scratch_shapes=scratch,
        compiler_params=pltpu.CompilerParams(
            dimension_semantics=("parallel", "parallel", "arbitrary"),
            vmem_limit_bytes=VMEM_LIMIT),
        name="hgrn2",
    )(proj_hg, proj_hg, proj_hg, proj_hg, hgrn_lb, norm_w)


def _rope(u, cos_t, sin_t):
    return u * cos_t + pltpu.roll(u, LANE // 2, axis=1) * sin_t


def _mla_prep_kernel(pm_ref, pos_ref, qnw_ref, kvnw_ref, wuq_ref, wukv_ref,
                     qhw_ref, khw_ref, invf_ref, sgn_ref, q_ref, k_ref, v_ref):
    cq = pm_ref[:, 0:Q_LORA]
    ckv = pm_ref[:, Q_LORA:Q_LORA + KV_LORA]
    kr = pm_ref[:, Q_LORA + KV_LORA:Q_LORA + KV_LORA + LANE]
    cqn = (cq * _rms_scale(cq) * qnw_ref[...]).astype(jnp.bfloat16)
    ckvn = (ckv * _rms_scale(ckv) * kvnw_ref[...]).astype(jnp.bfloat16)
    q = jnp.dot(cqn, wuq_ref[...], preferred_element_type=jnp.float32)
    kv = jnp.dot(ckvn, wukv_ref[...], preferred_element_type=jnp.float32)

    ang = pos_ref[...].astype(jnp.float32) * invf_ref[...]
    cos_t = jnp.cos(ang)
    sin_t = jnp.sin(ang) * sgn_ref[...]

    qhw = qhw_ref[...]
    khw = khw_ref[...]
    kr_ss = jnp.sum(kr * kr, axis=-1, keepdims=True)
    for h in range(MLA_HEADS):
        c0 = h * QK_PAD
        qn = q[:, c0:c0 + LANE]
        qr = q[:, c0 + LANE:c0 + QK_PAD]
        ss = jnp.sum(qn * qn, axis=-1, keepdims=True) + jnp.sum(qr * qr, axis=-1, keepdims=True)
        inv = lax.rsqrt(ss * (1.0 / QK_DIM) + NORM_EPS) * (ATTN_SCALE * LOG2_E)
        q_ref[:, c0:c0 + LANE] = (qn * inv * qhw[:, 0:LANE]).astype(q_ref.dtype)
        q_ref[:, c0 + LANE:c0 + QK_PAD] = _rope(qr * inv * qhw[:, LANE:QK_PAD],
                                                cos_t, sin_t).astype(q_ref.dtype)

        kn = kv[:, c0:c0 + LANE]
        ss = jnp.sum(kn * kn, axis=-1, keepdims=True) + kr_ss
        inv = lax.rsqrt(ss * (1.0 / QK_DIM) + NORM_EPS)
        k_ref[:, c0:c0 + LANE] = (kn * inv * khw[:, 0:LANE]).astype(k_ref.dtype)
        k_ref[:, c0 + LANE:c0 + QK_PAD] = _rope(kr * inv * khw[:, LANE:QK_PAD],
                                                cos_t, sin_t).astype(k_ref.dtype)
        v_ref[:, c0:c0 + V_DIM] = kv[:, c0 + LANE:c0 + QK_PAD].astype(v_ref.dtype)
        v_ref[:, c0 + V_DIM:c0 + 2 * V_DIM] = jnp.ones((v_ref.shape[0], V_DIM), v_ref.dtype)


def _mla_prep(proj_mla, pos, qnw, kvnw, wuq, wukv, qhw, khw, invf, sgn, *, tm):
    m = proj_mla.shape[0]
    full = lambda a: pl.BlockSpec(a.shape, lambda i: (0, 0))
    return pl.pallas_call(
        _mla_prep_kernel,
        grid=(m // tm,),
        in_specs=[pl.BlockSpec((tm, proj_mla.shape[1]), lambda i: (i, 0)),
                  pl.BlockSpec((tm, 1), lambda i: (i, 0)),
                  full(qnw), full(kvnw), full(wuq), full(wukv),
                  full(qhw), full(khw), full(invf), full(sgn)],
        out_specs=[pl.BlockSpec((tm, MLA_HEADS * QK_PAD), lambda i: (i, 0)),
                   pl.BlockSpec((tm, MLA_HEADS * QK_PAD), lambda i: (i, 0)),
                   pl.BlockSpec((tm, MLA_HEADS * 2 * V_DIM), lambda i: (i, 0))],
        out_shape=[jax.ShapeDtypeStruct((m, MLA_HEADS * QK_PAD), jnp.bfloat16),
                   jax.ShapeDtypeStruct((m, MLA_HEADS * QK_PAD), jnp.bfloat16),
                   jax.ShapeDtypeStruct((m, MLA_HEADS * 2 * V_DIM), jnp.bfloat16)],
        compiler_params=pltpu.CompilerParams(
            dimension_semantics=("parallel",),
            vmem_limit_bytes=VMEM_LIMIT),
        name="mla_prep",
    )(proj_mla, pos, qnw, kvnw, wuq, wukv, qhw, khw, invf, sgn)


def _attn_kernel(q_ref, k_ref, v_ref, o_ref, sa_ref, sb_ref, m_ref, acc_ref, *, tile):
    i = pl.program_id(2)
    n_col = tile // LANE
    m_ref[...] = jnp.full_like(m_ref, -jnp.inf)
    acc_ref[...] = jnp.zeros_like(acc_ref)

    def scores(j, s_ref):
        ks = pl.ds(pl.multiple_of(j * tile, tile), tile)
        s_ref[...] = lax.dot_general(q_ref[...], k_ref[ks, :], _NT,
                                     preferred_element_type=jnp.float32)

    def softmax_pv(j, s_ref, masked):
        ks = pl.ds(pl.multiple_of(j * tile, tile), tile)
        cols = [s_ref[:, c * LANE:(c + 1) * LANE] for c in range(n_col)]
        if masked:
            row_chunk = lax.broadcasted_iota(jnp.int32, (tile, LANE), 0) // CHUNK
            lane_chunk = lax.broadcasted_iota(jnp.int32, (tile, LANE), 1) // CHUNK
            cols = [jnp.where(lane_chunk + c * (LANE // CHUNK) <= row_chunk, s_c, -jnp.inf)
                    for c, s_c in enumerate(cols)]
        m_prev = m_ref[...]
        m_cur = functools.reduce(jnp.maximum, cols)
        m_new = jnp.maximum(m_prev, jnp.max(m_cur, axis=-1, keepdims=True))
        alpha = jnp.exp2(m_prev - m_new)
        p = jnp.concatenate([jnp.exp2(s_c - m_new) for s_c in cols], axis=1)
        pv = jnp.dot(p.astype(v_ref.dtype), v_ref[ks, :], preferred_element_type=jnp.float32)
        acc_ref[...] = jnp.concatenate([alpha, alpha], axis=1) * acc_ref[...] + pv
        m_ref[...] = m_new

    scores(0, sa_ref)

    def pair(jj, carry):
        j = 2 * jj
        scores(j + 1, sb_ref)
        softmax_pv(j, sa_ref, False)
        scores(j + 2, sa_ref)
        softmax_pv(j + 1, sb_ref, False)
        return carry

    lax.fori_loop(0, i // 2, pair, 0)

    @pl.when(i % 2 == 1)
    def _():
        scores(i, sb_ref)
        softmax_pv(i - 1, sa_ref, False)
        softmax_pv(i, sb_ref, True)

    @pl.when(i % 2 == 0)
    def _():
        softmax_pv(i, sa_ref, True)

    acc = acc_ref[...]
    o_ref[...] = (acc[:, :V_DIM] / acc[:, V_DIM:]).astype(o_ref.dtype)


def _attention(q, k, v1, *, batch, seq, tile):
    nq = seq // tile
    vw = 2 * V_DIM
    return pl.pallas_call(
        functools.partial(_attn_kernel, tile=tile),
        grid=(batch, MLA_HEADS, nq),
        in_specs=[pl.BlockSpec((tile, QK_PAD), lambda b, h, i: (b * nq + i, h)),
                  pl.BlockSpec((seq, QK_PAD), lambda b, h, i: (b, h)),
                  pl.BlockSpec((seq, vw), lambda b, h, i: (b, h))],
        out_specs=pl.BlockSpec((tile, V_DIM), lambda b, h, i: (b * nq + i, h)),
        out_shape=jax.ShapeDtypeStruct((batch * seq, MLA_HEADS * V_DIM), jnp.bfloat16),
        scratch_shapes=[pltpu.VMEM((tile, tile), jnp.float32),
                        pltpu.VMEM((tile, tile), jnp.float32),
                        pltpu.VMEM((tile, LANE), jnp.float32),
                        pltpu.VMEM((tile, vw), jnp.float32)],
        compiler_params=pltpu.CompilerParams(
            dimension_semantics=("parallel", "parallel", "arbitrary"),
            vmem_limit_bytes=VMEM_LIMIT),
        name="mla_attention",
    )(q, k, v1)


def _rope_lanes(a):
    half = QK_ROPE // 2
    z = jnp.zeros(a.shape[:-1] + (LANE // 2 - half,), a.dtype)
    return jnp.concatenate([a[..., :half], z, a[..., half:], z], axis=-1)


def _head_lanes(a):
    return jnp.concatenate([a[..., :QK_NOPE], _rope_lanes(a[..., QK_NOPE:])], axis=-1)


def kernel(x, positions, attn_norm_w, w_in, hgrn_lb, hgrn_norm_w, mla_q_norm_w, w_uq,
           mla_kv_norm_w, w_ukv, q_head_norm_w, k_head_norm_w, w_out, ffn_norm_w,
           w_gate_up, w_down):
    batch, seq, d = x.shape
    m = batch * seq
    bf = jnp.bfloat16
    layer = 0
    xf = x.reshape(m, d)

    hg_cols = 4 * HG_WIDTH
    w_in_l = w_in[layer]
    w_hg = w_in_l[:, :hg_cols].astype(bf)
    w_mla = jnp.concatenate(
        [w_in_l[:, hg_cols:hg_cols + Q_LORA + KV_LORA],
         _rope_lanes(w_in_l[:, hg_cols + Q_LORA + KV_LORA:])], axis=-1).astype(bf)
    wuq = _head_lanes(w_uq[layer].reshape(Q_LORA, MLA_HEADS, QK_DIM)).reshape(
        Q_LORA, MLA_HEADS * QK_PAD).astype(bf)
    wukv = w_ukv[layer].astype(bf)
    qhw = _head_lanes(q_head_norm_w[layer])[None, :]
    khw = _head_lanes(k_head_norm_w[layer])[None, :]
    inv_freq = 1.0 / (ROPE_THETA ** (jnp.arange(0, QK_ROPE, 2, dtype=jnp.float32) / QK_ROPE))
    invf = _rope_lanes(jnp.concatenate([inv_freq, inv_freq]))[None, :]
    half = QK_ROPE // 2
    sgn = _rope_lanes(jnp.concatenate([-jnp.ones((half,), jnp.float32),
                                       jnp.ones((half,), jnp.float32)]))[None, :]
    nw_attn = attn_norm_w[layer][None, :]

    proj_hg = _norm_matmul(xf, nw_attn, w_hg, tm=1024, tn=512,
                           out_dtype=jnp.float32, name="in_proj_hgrn")
    proj_mla = _norm_matmul(xf, nw_attn, w_mla, tm=1024, tn=w_mla.shape[1],
                            out_dtype=jnp.float32, name="in_proj_mla")

    o_a = _hgrn(proj_hg, hgrn_lb, hgrn_norm_w[layer][None, :],
                batch=batch, seq=seq, rows=512)

    q, k, v = _mla_prep(proj_mla, positions.reshape(m, 1),
                        mla_q_norm_w[layer][None, :], mla_kv_norm_w[layer][None, :],
                        wuq, wukv, qhw, khw, invf, sgn, tm=256)
    o_b = _attention(q, k, v, batch=batch, seq=seq, tile=512)

    x1 = _res_matmul([o_a, o_b], w_out[layer].astype(bf), xf, tm=1024, tn=512, name="out_proj")

    act = _norm_swiglu(x1, ffn_norm_w[layer][None, :], w_gate_up[layer].astype(bf),
                       tm=1024, tn=512)
    x2 = _res_matmul([act], w_down[layer].astype(bf), x1, tm=512, tn=512, name="ffn_down")
    return x2.reshape(batch, seq, d)
```

```python
import functools

import jax
import jax.numpy as jnp
import numpy as np
from jax import lax
from jax.experimental import pallas as pl
from jax.experimental.pallas import tpu as pltpu

D_MODEL = 2048
CHUNK = 64
SUB = 16
GROUP = 16
HG_WIDTH = 1024
HG_HEADS = 8
HG_D = 128
MLA_HEADS = 8
V_DIM = 128
QK_NOPE = 128
QK_ROPE = 64
QK_DIM = QK_NOPE + QK_ROPE
QK_PAD = 256
Q_LORA = 512
KV_LORA = 512
ROPE_THETA = 10000.0
ATTN_SCALE = QK_DIM ** -0.5
LOG2_E = 1.4426950408889634
D_FF = 5632
NORM_EPS = 1e-6
LANE = 128
MLA_COLS = Q_LORA + KV_LORA + LANE
PROJ_COLS = 5376

VMEM_LIMIT = 56 * 1024 * 1024

_NT = (((1,), (1,)), ((), ()))


def _silu(x):
    return x * (1.0 / (1.0 + jnp.exp(-x)))


def _rms_scale(x):
    return lax.rsqrt(jnp.mean(x * x, axis=-1, keepdims=True) + NORM_EPS)


def _norm_matmul_kernel(x_ref, nw_ref, w_ref, o_ref, h_ref):
    @pl.when(pl.program_id(1) == 0)
    def _():
        x = x_ref[...]
        h_ref[...] = (x * _rms_scale(x) * nw_ref[...]).astype(h_ref.dtype)

    o_ref[...] = jnp.dot(h_ref[...], w_ref[...],
                         preferred_element_type=jnp.float32).astype(o_ref.dtype)


def _norm_matmul(x, nw, w, *, tm, tn, out_dtype, name):
    m, k = x.shape
    n = w.shape[1]
    return pl.pallas_call(
        _norm_matmul_kernel,
        grid=(m // tm, n // tn),
        in_specs=[pl.BlockSpec((tm, k), lambda i, j: (i, 0)),
                  pl.BlockSpec((1, k), lambda i, j: (0, 0)),
                  pl.BlockSpec((k, tn), lambda i, j: (0, j))],
        out_specs=pl.BlockSpec((tm, tn), lambda i, j: (i, j)),
        out_shape=jax.ShapeDtypeStruct((m, n), out_dtype),
        scratch_shapes=[pltpu.VMEM((tm, k), jnp.bfloat16)],
        compiler_params=pltpu.CompilerParams(
            dimension_semantics=("parallel", "arbitrary"),
            vmem_limit_bytes=VMEM_LIMIT),
        name=name,
    )(x, nw, w)


def _norm_swiglu_kernel(x_ref, nw_ref, wg_ref, wu_ref, o_ref, h_ref):
    @pl.when(pl.program_id(1) == 0)
    def _():
        x = x_ref[...]
        h_ref[...] = (x * _rms_scale(x) * nw_ref[...]).astype(h_ref.dtype)

    h = h_ref[...]
    g = jnp.dot(h, wg_ref[...], preferred_element_type=jnp.float32)
    u = jnp.dot(h, wu_ref[...], preferred_element_type=jnp.float32)
    o_ref[...] = (_silu(g) * u).astype(o_ref.dtype)


def _norm_swiglu(x, nw, w_gate_up, *, tm, tn):
    m, k = x.shape
    n = w_gate_up.shape[1] // 2
    up_off = n // tn
    return pl.pallas_call(
        _norm_swiglu_kernel,
        grid=(m // tm, n // tn),
        in_specs=[pl.BlockSpec((tm, k), lambda i, j: (i, 0)),
                  pl.BlockSpec((1, k), lambda i, j: (0, 0)),
                  pl.BlockSpec((k, tn), lambda i, j: (0, j)),
                  pl.BlockSpec((k, tn), lambda i, j: (0, j + up_off))],
        out_specs=pl.BlockSpec((tm, tn), lambda i, j: (i, j)),
        out_shape=jax.ShapeDtypeStruct((m, n), jnp.bfloat16),
        scratch_shapes=[pltpu.VMEM((tm, k), jnp.bfloat16)],
        compiler_params=pltpu.CompilerParams(
            dimension_semantics=("parallel", "arbitrary"),
            vmem_limit_bytes=VMEM_LIMIT),
        name="ffn_up",
    )(x, nw, w_gate_up, w_gate_up)


def _res_matmul_kernel(*refs, n_a):
    a_refs = refs[:n_a]
    w_refs = refs[n_a:2 * n_a]
    r_ref = refs[2 * n_a]
    o_ref = refs[2 * n_a + 1]
    acc = r_ref[...]
    for a_ref, w_ref in zip(a_refs, w_refs):
        acc = acc + jnp.dot(a_ref[...], w_ref[...], preferred_element_type=jnp.float32)
    o_ref[...] = acc


def _res_matmul(a_list, w, res, *, tm, tn, name):
    m, n = res.shape
    n_a = len(a_list)
    ka = a_list[0].shape[1]
    in_specs = [pl.BlockSpec((tm, ka), lambda i, j: (i, 0)) for _ in a_list]
    in_specs += [pl.BlockSpec((ka, tn), functools.partial(lambda i, j, kb: (kb, j), kb=kb))
                 for kb in range(n_a)]
    in_specs += [pl.BlockSpec((tm, tn), lambda i, j: (i, j))]
    return pl.pallas_call(
        functools.partial(_res_matmul_kernel, n_a=n_a),
        grid=(m // tm, n // tn),
        in_specs=in_specs,
        out_specs=pl.BlockSpec((tm, tn), lambda i, j: (i, j)),
        out_shape=jax.ShapeDtypeStruct((m, n), jnp.float32),
        compiler_params=pltpu.CompilerParams(
            dimension_semantics=("parallel", "arbitrary"),
            vmem_limit_bytes=VMEM_LIMIT),
        name=name,
    )(*a_list, *([w] * n_a), res)


def _segmented_cumsum(x, row, seg):
    pos = row % seg
    shift = 1
    while shift < seg:
        x = x + jnp.where(pos >= shift, pltpu.roll(x, shift, axis=0), 0.0)
        shift *= 2
    return x


def _hgrn_kernel(q_ref, f_ref, i_ref, g_ref, lb_ref, nw_ref, o_ref,
                 state_ref, qf_ref, qt_ref, k_ref, b_ref, oacc_ref, *, rows):
    n_chunks = rows // CHUNK
    n_sub = CHUNK // SUB

    @pl.when(pl.program_id(2) == 0)
    def _():
        state_ref[...] = jnp.zeros_like(state_ref)

    lbr = lb_ref[...]
    lbe = jnp.exp(lbr - jnp.max(lbr, axis=0, keepdims=True))
    lb = lbe[0:1, :] / jnp.sum(lbe, axis=0, keepdims=True)

    qf = _silu(q_ref[...].astype(jnp.float32))
    f = lb + (1.0 - lb) * (1.0 / (1.0 + jnp.exp(-f_ref[...].astype(jnp.float32))))
    row = lax.broadcasted_iota(jnp.int32, (rows, HG_D), 0)
    b = _segmented_cumsum(jnp.log2(f), row, SUB)
    qf_ref[...] = qf
    qt_ref[...] = qf * jnp.exp2(b)
    k_ref[...] = 1.0 - f
    b_ref[...] = b

    r_i = lax.broadcasted_iota(jnp.int32, (CHUNK, CHUNK), 0)
    c_i = lax.broadcasted_iota(jnp.int32, (CHUNK, CHUNK), 1)
    causal = c_i <= r_i
    off_diag = c_i < (r_i // SUB) * SUB
    lane8 = lax.broadcasted_iota(jnp.int32, (8, CHUNK), 1)
    lane_bits = [(lane8 & (1 << n)) != 0 for n in range(3)]
    mid_block = ((r_i // 8) % 2 == 1) & (c_i // 8 == r_i // 8 - 1)

    def chunk_matmuls(c):
        r0 = pl.multiple_of(c * CHUNK, CHUNK)
        sl = pl.ds(r0, CHUNK)
        vc = i_ref[sl, :]
        qt_c = qt_ref[sl, :]

        q_blk, k_blk, b_blk, tot, khat = [], [], [], [], []
        for blk in range(n_sub):
            bs = pl.ds(pl.multiple_of(r0 + blk * SUB, SUB), SUB)
            q_blk.append(qf_ref[bs, :])
            k_blk.append(k_ref[bs, :])
            b_blk.append(b_ref[bs, :])
            tot.append(b_ref[pl.ds(r0 + blk * SUB + SUB - 1, 1), :])
            khat.append(k_blk[blk] * jnp.exp2(tot[blk] - b_blk[blk]))

        def span(lo, hi):
            return functools.reduce(lambda a, t: a + t, tot[lo + 1:hi], tot[lo])

        def keys_seen_from(blk):
            parts = [khat[j] * jnp.exp2(span(j + 1, blk)) if j < blk - 1 else khat[j]
                     for j in range(n_sub)]
            return jnp.concatenate(parts, axis=0)

        k_stack = jnp.concatenate([keys_seen_from(blk) for blk in range(1, n_sub)],
                                  axis=0).astype(jnp.bfloat16)
        s_all = lax.dot_general(qt_c.astype(jnp.bfloat16), k_stack, _NT,
                                preferred_element_type=jnp.float32)

        q_mid, k_mid = [], []
        for blk in range(n_sub):
            mid = b_ref[pl.ds(r0 + blk * SUB + 7, 1), :]
            q_mid += [q_blk[blk][0:8, :], q_blk[blk][8:16, :] * jnp.exp2(b_blk[blk][8:16, :] - mid)]
            k_mid += [k_blk[blk][0:8, :] * jnp.exp2(mid - b_blk[blk][0:8, :]), k_blk[blk][8:16, :]]
        s_mid = lax.dot_general(jnp.concatenate(q_mid, axis=0).astype(jnp.bfloat16),
                                jnp.concatenate(k_mid, axis=0).astype(jnp.bfloat16), _NT,
                                preferred_element_type=jnp.float32)

        upd = jnp.dot(vc.astype(jnp.float32).T.astype(jnp.bfloat16),
                      keys_seen_from(n_sub).astype(jnp.bfloat16),
                      preferred_element_type=jnp.float32)

        q_state = jnp.concatenate(
            [qt_c[0:SUB, :]] + [qt_c[blk * SUB:(blk + 1) * SUB, :] * jnp.exp2(span(0, blk))
                                for blk in range(1, n_sub)], axis=0).astype(jnp.bfloat16)
        return dict(r0=r0, sl=sl, vc=vc, q_blk=q_blk, b_blk=b_blk,
                    s_all=s_all, s_mid=s_mid, upd=upd, q_state=q_state,
                    decay=jnp.exp2(span(0, n_sub)))

    def chunk_scores(w):
        r0 = w["r0"]
        off_rows = [jnp.zeros((SUB, CHUNK), jnp.float32)]
        for blk in range(1, n_sub):
            off_rows.append(w["s_all"][blk * SUB:(blk + 1) * SUB, (blk - 1) * CHUNK:blk * CHUNK])
        s_off = jnp.concatenate(off_rows, axis=0)

        diag_rows = []
        for blk in range(n_sub):
            for half in range(SUB // 8):
                q_h = w["q_blk"][blk][half * 8:(half + 1) * 8, :]
                b_h = w["b_blk"][blk][half * 8:(half + 1) * 8, :]
                cols = []
                for s in range(half * 8, (half + 1) * 8):
                    row_s = pl.ds(r0 + blk * SUB + s, 1)
                    d = b_h - b_ref[row_s, :]
                    if s > half * 8:
                        d = jnp.minimum(d, 0.0)
                    cols.append(jnp.sum(q_h * k_ref[row_s, :] * jnp.exp2(d),
                                        axis=-1, keepdims=True))
                bit = 0
                while len(cols) > 1:
                    cols = [jnp.where(lane_bits[bit], cols[n + 1], cols[n])
                            for n in range(0, len(cols), 2)]
                    bit += 1
                diag_rows.append(cols[0])
        s_diag = jnp.concatenate(diag_rows, axis=0)

        return jnp.where(
            causal, jnp.where(off_diag, s_off, jnp.where(mid_block, w["s_mid"], s_diag)),
            0.0).astype(jnp.bfloat16)

    def group_body(g, carry):
        work = [chunk_matmuls(g * GROUP + n) for n in range(GROUP)]
        scores = [chunk_scores(w) for w in work]
        state_t = state_ref[...]
        for w, sc in zip(work, scores):
            o_c = jnp.dot(sc, w["vc"], preferred_element_type=jnp.float32)
            o_c = o_c + lax.dot_general(w["q_state"], state_t.astype(jnp.bfloat16), _NT,
                                        preferred_element_type=jnp.float32)
            oacc_ref[w["sl"], :] = o_c
            state_t = state_t * w["decay"] + w["upd"]
        state_ref[...] = state_t
        return carry

    lax.fori_loop(0, n_chunks // GROUP, group_body, 0)

    o = oacc_ref[...]
    o = o * _rms_scale(o) * nw_ref[...]
    o_ref[...] = (o * _silu(g_ref[...].astype(jnp.float32))).astype(o_ref.dtype)


def _hgrn(proj, hgrn_lb, norm_w, *, batch, seq, rows):
    n_row_blocks = seq // rows
    hb = HG_HEADS
    col0 = MLA_COLS // HG_D

    def col_spec(group):
        return pl.BlockSpec((rows, HG_D),
                            lambda b, h, t: (b * n_row_blocks + t, col0 + group * hb + h))

    scratch = [pltpu.VMEM((HG_D, HG_D), jnp.float32)]
    scratch += [pltpu.VMEM((rows, HG_D), jnp.float32) for _ in range(5)]
    return pl.pallas_call(
        functools.partial(_hgrn_kernel, rows=rows),
        grid=(batch, hb, n_row_blocks),
        in_specs=[col_spec(0), col_spec(1), col_spec(2), col_spec(3),
                  pl.BlockSpec((hgrn_lb.shape[0], HG_D), lambda b, h, t: (0, h)),
                  pl.BlockSpec((1, HG_D), lambda b, h, t: (0, 0))],
        out_specs=pl.BlockSpec((rows, HG_D), lambda b, h, t: (b * n_row_blocks + t, h)),
        out_shape=jax.ShapeDtypeStruct((batch * seq, HG_WIDTH), jnp.bfloat16),
        scratch_shapes=scratch,
        compiler_params=pltpu.CompilerParams(
            dimension_semantics=("parallel", "parallel", "arbitrary"),
            vmem_limit_bytes=VMEM_LIMIT),
        name="hgrn2",
    )(proj, proj, proj, proj, hgrn_lb, norm_w)


def _rope(u, cos_t, sin_t):
    return u * cos_t + pltpu.roll(u, LANE // 2, axis=1) * sin_t


def _mla_prep_kernel(pm_ref, pos_ref, qnw_ref, kvnw_ref, wuq_ref, wukv_ref,
                     qhw_ref, khw_ref, invf_ref, sgn_ref, q_ref, k_ref, v_ref):
    cq = pm_ref[:, 0:Q_LORA].astype(jnp.float32)
    ckv = pm_ref[:, Q_LORA:Q_LORA + KV_LORA].astype(jnp.float32)
    kr = pm_ref[:, Q_LORA + KV_LORA:MLA_COLS].astype(jnp.float32)
    cqn = (cq * _rms_scale(cq) * qnw_ref[...]).astype(jnp.bfloat16)
    ckvn = (ckv * _rms_scale(ckv) * kvnw_ref[...]).astype(jnp.bfloat16)
    q = jnp.dot(cqn, wuq_ref[...], preferred_element_type=jnp.float32)
    kv = jnp.dot(ckvn, wukv_ref[...], preferred_element_type=jnp.float32)

    ang = pos_ref[...].astype(jnp.float32) * invf_ref[...]
    cos_t = jnp.cos(ang)
    sin_t = jnp.sin(ang) * sgn_ref[...]

    qhw = qhw_ref[...]
    khw = khw_ref[...]
    kr_ss = jnp.sum(kr * kr, axis=-1, keepdims=True)
    for h in range(MLA_HEADS):
        c0 = h * QK_PAD
        qn = q[:, c0:c0 + LANE]
        qr = q[:, c0 + LANE:c0 + QK_PAD]
        ss = jnp.sum(qn * qn, axis=-1, keepdims=True) + jnp.sum(qr * qr, axis=-1, keepdims=True)
        inv = lax.rsqrt(ss * (1.0 / QK_DIM) + NORM_EPS) * (ATTN_SCALE * LOG2_E)
        q_ref[:, c0:c0 + LANE] = (qn * inv * qhw[:, 0:LANE]).astype(q_ref.dtype)
        q_ref[:, c0 + LANE:c0 + QK_PAD] = _rope(qr * inv * qhw[:, LANE:QK_PAD],
                                                cos_t, sin_t).astype(q_ref.dtype)

        kn = kv[:, c0:c0 + LANE]
        ss = jnp.sum(kn * kn, axis=-1, keepdims=True) + kr_ss
        inv = lax.rsqrt(ss * (1.0 / QK_DIM) + NORM_EPS)
        k_ref[:, c0:c0 + LANE] = (kn * inv * khw[:, 0:LANE]).astype(k_ref.dtype)
        k_ref[:, c0 + LANE:c0 + QK_PAD] = _rope(kr * inv * khw[:, LANE:QK_PAD],
                                                cos_t, sin_t).astype(k_ref.dtype)
        v_ref[:, c0:c0 + V_DIM] = kv[:, c0 + LANE:c0 + QK_PAD].astype(v_ref.dtype)
        v_ref[:, c0 + V_DIM:c0 + 2 * V_DIM] = jnp.ones((v_ref.shape[0], V_DIM), v_ref.dtype)


def _mla_prep(proj_mla, pos, qnw, kvnw, wuq, wukv, qhw, khw, invf, sgn, *, tm):
    m = proj_mla.shape[0]
    full = lambda a: pl.BlockSpec(a.shape, lambda i: (0, 0))
    return pl.pallas_call(
        _mla_prep_kernel,
        grid=(m // tm,),
        in_specs=[pl.BlockSpec((tm, MLA_COLS), lambda i: (i, 0)),
                  pl.BlockSpec((tm, 1), lambda i: (i, 0)),
                  full(qnw), full(kvnw), full(wuq), full(wukv),
                  full(qhw), full(khw), full(invf), full(sgn)],
        out_specs=[pl.BlockSpec((tm, MLA_HEADS * QK_PAD), lambda i: (i, 0)),
                   pl.BlockSpec((tm, MLA_HEADS * QK_PAD), lambda i: (i, 0)),
                   pl.BlockSpec((tm, MLA_HEADS * 2 * V_DIM), lambda i: (i, 0))],
        out_shape=[jax.ShapeDtypeStruct((m, MLA_HEADS * QK_PAD), jnp.bfloat16),
                   jax.ShapeDtypeStruct((m, MLA_HEADS * QK_PAD), jnp.bfloat16),
                   jax.ShapeDtypeStruct((m, MLA_HEADS * 2 * V_DIM), jnp.bfloat16)],
        compiler_params=pltpu.CompilerParams(
            dimension_semantics=("parallel",),
            vmem_limit_bytes=VMEM_LIMIT),
        name="mla_prep",
    )(proj_mla, pos, qnw, kvnw, wuq, wukv, qhw, khw, invf, sgn)


def _attn_kernel(q_ref, k_ref, v_ref, o_ref, sa_ref, sb_ref, m_ref, acc_ref, *, tile):
    i = pl.program_id(2)
    n_col = tile // LANE
    m_ref[...] = jnp.full_like(m_ref, -jnp.inf)
    acc_ref[...] = jnp.zeros_like(acc_ref)

    def scores(j, s_ref):
        ks = pl.ds(pl.multiple_of(j * tile, tile), tile)
        s_ref[...] = lax.dot_general(q_ref[...], k_ref[ks, :], _NT,
                                     preferred_element_type=jnp.float32)

    def softmax_pv(j, s_ref, masked):
        ks = pl.ds(pl.multiple_of(j * tile, tile), tile)
        cols = [s_ref[:, c * LANE:(c + 1) * LANE] for c in range(n_col)]
        if masked:
            row_chunk = lax.broadcasted_iota(jnp.int32, (tile, LANE), 0) // CHUNK
            lane_chunk = lax.broadcasted_iota(jnp.int32, (tile, LANE), 1) // CHUNK
            cols = [jnp.where(lane_chunk + c * (LANE // CHUNK) <= row_chunk, s_c, -jnp.inf)
                    for c, s_c in enumerate(cols)]
        m_prev = m_ref[...]
        m_cur = functools.reduce(jnp.maximum, cols)
        m_new = jnp.maximum(m_prev, jnp.max(m_cur, axis=-1, keepdims=True))
        alpha = jnp.exp2(m_prev - m_new)
        p = jnp.concatenate([jnp.exp2(s_c - m_new) for s_c in cols], axis=1)
        pv = jnp.dot(p.astype(v_ref.dtype), v_ref[ks, :], preferred_element_type=jnp.float32)
        acc_ref[...] = jnp.concatenate([alpha, alpha], axis=1) * acc_ref[...] + pv
        m_ref[...] = m_new

    scores(0, sa_ref)

    def pair(jj, carry):
        j = 2 * jj
        scores(j + 1, sb_ref)
        softmax_pv(j, sa_ref, False)
        scores(j + 2, sa_ref)
        softmax_pv(j + 1, sb_ref, False)
        return carry

    lax.fori_loop(0, i // 2, pair, 0)

    @pl.when(i % 2 == 1)
    def _():
        scores(i, sb_ref)
        softmax_pv(i - 1, sa_ref, False)
        softmax_pv(i, sb_ref, True)

    @pl.when(i % 2 == 0)
    def _():
        softmax_pv(i, sa_ref, True)

    acc = acc_ref[...]
    o_ref[...] = (acc[:, :V_DIM] / acc[:, V_DIM:]).astype(o_ref.dtype)


def _attention(q, k, v1, *, batch, seq, tile):
    nq = seq // tile
    vw = 2 * V_DIM
    return pl.pallas_call(
        functools.partial(_attn_kernel, tile=tile),
        grid=(batch, MLA_HEADS, nq),
        in_specs=[pl.BlockSpec((tile, QK_PAD), lambda b, h, i: (b * nq + i, h)),
                  pl.BlockSpec((seq, QK_PAD), lambda b, h, i: (b, h)),
                  pl.BlockSpec((seq, vw), lambda b, h, i: (b, h))],
        out_specs=pl.BlockSpec((tile, V_DIM), lambda b, h, i: (b * nq + i, h)),
        out_shape=jax.ShapeDtypeStruct((batch * seq, MLA_HEADS * V_DIM), jnp.bfloat16),
        scratch_shapes=[pltpu.VMEM((tile, tile), jnp.float32),
                        pltpu.VMEM((tile, tile), jnp.float32),
                        pltpu.VMEM((tile, LANE), jnp.float32),
                        pltpu.VMEM((tile, vw), jnp.float32)],
        compiler_params=pltpu.CompilerParams(
            dimension_semantics=("parallel", "parallel", "arbitrary"),
            vmem_limit_bytes=VMEM_LIMIT),
        name="mla_attention",
    )(q, k, v1)


def _rope_lanes(a):
    half = QK_ROPE // 2
    z = jnp.zeros(a.shape[:-1] + (LANE // 2 - half,), a.dtype)
    return jnp.concatenate([a[..., :half], z, a[..., half:], z], axis=-1)


def _head_lanes(a):
    return jnp.concatenate([a[..., :QK_NOPE], _rope_lanes(a[..., QK_NOPE:])], axis=-1)


def kernel(x, positions, attn_norm_w, w_in, hgrn_lb, hgrn_norm_w, mla_q_norm_w, w_uq,
           mla_kv_norm_w, w_ukv, q_head_norm_w, k_head_norm_w, w_out, ffn_norm_w,
           w_gate_up, w_down):
    batch, seq, d = x.shape
    m = batch * seq
    bf = jnp.bfloat16
    layer = 0
    xf = x.reshape(m, d)

    hg_cols = 4 * HG_WIDTH
    w_in_l = w_in[layer]
    w_proj = jnp.concatenate(
        [w_in_l[:, hg_cols:hg_cols + Q_LORA + KV_LORA],
         _rope_lanes(w_in_l[:, hg_cols + Q_LORA + KV_LORA:]),
         w_in_l[:, :hg_cols],
         jnp.zeros((d, PROJ_COLS - MLA_COLS - hg_cols), w_in_l.dtype)], axis=-1).astype(bf)
    wuq = _head_lanes(w_uq[layer].reshape(Q_LORA, MLA_HEADS, QK_DIM)).reshape(
        Q_LORA, MLA_HEADS * QK_PAD).astype(bf)
    wukv = w_ukv[layer].astype(bf)
    qhw = _head_lanes(q_head_norm_w[layer])[None, :]
    khw = _head_lanes(k_head_norm_w[layer])[None, :]
    inv_freq = 1.0 / (ROPE_THETA ** (jnp.arange(0, QK_ROPE, 2, dtype=jnp.float32) / QK_ROPE))
    invf = _rope_lanes(jnp.concatenate([inv_freq, inv_freq]))[None, :]
    half = QK_ROPE // 2
    sgn = _rope_lanes(jnp.concatenate([-jnp.ones((half,), jnp.float32),
                                       jnp.ones((half,), jnp.float32)]))[None, :]
    nw_attn = attn_norm_w[layer][None, :]

    proj = _norm_matmul(xf, nw_attn, w_proj, tm=1024, tn=PROJ_COLS // 6,
                        out_dtype=bf, name="in_proj")

    o_a = _hgrn(proj, hgrn_lb, hgrn_norm_w[layer][None, :],
                batch=batch, seq=seq, rows=1024)

    q, k, v = _mla_prep(proj, positions.reshape(m, 1),
                        mla_q_norm_w[layer][None, :], mla_kv_norm_w[layer][None, :],
                        wuq, wukv, qhw, khw, invf, sgn, tm=256)
    o_b = _attention(q, k, v, batch=batch, seq=seq, tile=512)

    x1 = _res_matmul([o_a, o_b], w_out[layer].astype(bf), xf, tm=1024, tn=512, name="out_proj")

    act = _norm_swiglu(x1, ffn_norm_w[layer][None, :], w_gate_up[layer].astype(bf),
                       tm=1024, tn=512)
    x2 = _res_matmul([act], w_down[layer].astype(bf), x1, tm=1024, tn=512, name="ffn_down")
    return x2.reshape(batch, seq, d)
```

```python
import functools

import jax
import jax.numpy as jnp
import numpy as np
from jax import lax
from jax.experimental import pallas as pl
from jax.experimental.pallas import tpu as pltpu

D_MODEL = 2048
CHUNK = 64
SUB = 16
GROUP = 16
HG_WIDTH = 1024
HG_HEADS = 8
HG_D = 128
MLA_HEADS = 8
V_DIM = 128
QK_NOPE = 128
QK_ROPE = 64
QK_DIM = QK_NOPE + QK_ROPE
QK_PAD = 256
Q_LORA = 512
KV_LORA = 512
ROPE_THETA = 10000.0
ATTN_SCALE = QK_DIM ** -0.5
LOG2_E = 1.4426950408889634
D_FF = 5632
NORM_EPS = 1e-6
LANE = 128
PROJ_COLS = 5376

VMEM_LIMIT = 56 * 1024 * 1024

_NT = (((1,), (1,)), ((), ()))


def _silu(x):
    return x * (1.0 / (1.0 + jnp.exp(-x)))


def _rms_scale(x):
    return lax.rsqrt(jnp.mean(x * x, axis=-1, keepdims=True) + NORM_EPS)


def _norm_matmul_kernel(x_ref, nw_ref, w_ref, o_ref, h_ref):
    @pl.when(pl.program_id(1) == 0)
    def _():
        x = x_ref[...]
        h_ref[...] = (x * _rms_scale(x) * nw_ref[...]).astype(h_ref.dtype)

    o_ref[...] = jnp.dot(h_ref[...], w_ref[...],
                         preferred_element_type=jnp.float32).astype(o_ref.dtype)


def _norm_matmul(x, nw, w, *, tm, tn, out_dtype, name):
    m, k = x.shape
    n = w.shape[1]
    return pl.pallas_call(
        _norm_matmul_kernel,
        grid=(m // tm, n // tn),
        in_specs=[pl.BlockSpec((tm, k), lambda i, j: (i, 0)),
                  pl.BlockSpec((1, k), lambda i, j: (0, 0)),
                  pl.BlockSpec((k, tn), lambda i, j: (0, j))],
        out_specs=pl.BlockSpec((tm, tn), lambda i, j: (i, j)),
        out_shape=jax.ShapeDtypeStruct((m, n), out_dtype),
        scratch_shapes=[pltpu.VMEM((tm, k), jnp.bfloat16)],
        compiler_params=pltpu.CompilerParams(
            dimension_semantics=("parallel", "arbitrary"),
            vmem_limit_bytes=VMEM_LIMIT),
        name=name,
    )(x, nw, w)


def _norm_swiglu_kernel(x_ref, nw_ref, wg_ref, wu_ref, o_ref, h_ref):
    @pl.when(pl.program_id(1) == 0)
    def _():
        x = x_ref[...]
        h_ref[...] = (x * _rms_scale(x) * nw_ref[...]).astype(h_ref.dtype)

    h = h_ref[...]
    g = jnp.dot(h, wg_ref[...], preferred_element_type=jnp.float32)
    u = jnp.dot(h, wu_ref[...], preferred_element_type=jnp.float32)
    o_ref[...] = (_silu(g) * u).astype(o_ref.dtype)


def _norm_swiglu(x, nw, w_gate_up, *, tm, tn):
    m, k = x.shape
    n = w_gate_up.shape[1] // 2
    up_off = n // tn
    return pl.pallas_call(
        _norm_swiglu_kernel,
        grid=(m // tm, n // tn),
        in_specs=[pl.BlockSpec((tm, k), lambda i, j: (i, 0)),
                  pl.BlockSpec((1, k), lambda i, j: (0, 0)),
                  pl.BlockSpec((k, tn), lambda i, j: (0, j)),
                  pl.BlockSpec((k, tn), lambda i, j: (0, j + up_off))],
        out_specs=pl.BlockSpec((tm, tn), lambda i, j: (i, j)),
        out_shape=jax.ShapeDtypeStruct((m, n), jnp.bfloat16),
        scratch_shapes=[pltpu.VMEM((tm, k), jnp.bfloat16)],
        compiler_params=pltpu.CompilerParams(
            dimension_semantics=("parallel", "arbitrary"),
            vmem_limit_bytes=VMEM_LIMIT),
        name="ffn_up",
    )(x, nw, w_gate_up, w_gate_up)


def _res_matmul_kernel(*refs, n_a):
    a_refs = refs[:n_a]
    w_refs = refs[n_a:2 * n_a]
    r_ref = refs[2 * n_a]
    o_ref = refs[2 * n_a + 1]
    acc = r_ref[...]
    for a_ref, w_ref in zip(a_refs, w_refs):
        acc = acc + jnp.dot(a_ref[...], w_ref[...], preferred_element_type=jnp.float32)
    o_ref[...] = acc


def _res_matmul(a_list, w, res, *, tm, tn, name):
    m, n = res.shape
    n_a = len(a_list)
    ka = a_list[0].shape[1]
    in_specs = [pl.BlockSpec((tm, ka), lambda i, j: (i, 0)) for _ in a_list]
    in_specs += [pl.BlockSpec((ka, tn), functools.partial(lambda i, j, kb: (kb, j), kb=kb))
                 for kb in range(n_a)]
    in_specs += [pl.BlockSpec((tm, tn), lambda i, j: (i, j))]
    return pl.pallas_call(
        functools.partial(_res_matmul_kernel, n_a=n_a),
        grid=(m // tm, n // tn),
        in_specs=in_specs,
        out_specs=pl.BlockSpec((tm, tn), lambda i, j: (i, j)),
        out_shape=jax.ShapeDtypeStruct((m, n), jnp.float32),
        compiler_params=pltpu.CompilerParams(
            dimension_semantics=("parallel", "arbitrary"),
            vmem_limit_bytes=VMEM_LIMIT),
        name=name,
    )(*a_list, *([w] * n_a), res)


def _segmented_cumsum(x, row, seg):
    pos = row % seg
    shift = 1
    while shift < seg:
        x = x + jnp.where(pos >= shift, pltpu.roll(x, shift, axis=0), 0.0)
        shift *= 2
    return x


def _hgrn_kernel(q_ref, f_ref, i_ref, g_ref, lb_ref, nw_ref, o_ref,
                 state_ref, qf_ref, qt_ref, k_ref, b_ref, oacc_ref, *, rows):
    n_chunks = rows // CHUNK
    n_sub = CHUNK // SUB

    @pl.when(pl.program_id(2) == 0)
    def _():
        state_ref[...] = jnp.zeros_like(state_ref)

    lbr = lb_ref[...]
    lbe = jnp.exp(lbr - jnp.max(lbr, axis=0, keepdims=True))
    lb = lbe[0:1, :] / jnp.sum(lbe, axis=0, keepdims=True)

    qf = _silu(q_ref[...].astype(jnp.float32))
    f = lb + (1.0 - lb) * (1.0 / (1.0 + jnp.exp(-f_ref[...].astype(jnp.float32))))
    row = lax.broadcasted_iota(jnp.int32, (rows, HG_D), 0)
    b = _segmented_cumsum(jnp.log2(f), row, SUB)
    qf_ref[...] = qf
    qt_ref[...] = qf * jnp.exp2(b)
    k_ref[...] = 1.0 - f
    b_ref[...] = b

    r_i = lax.broadcasted_iota(jnp.int32, (CHUNK, CHUNK), 0)
    c_i = lax.broadcasted_iota(jnp.int32, (CHUNK, CHUNK), 1)
    causal = c_i <= r_i
    off_diag = c_i < (r_i // SUB) * SUB
    lane8 = lax.broadcasted_iota(jnp.int32, (8, CHUNK), 1)
    lane_bits = [(lane8 & (1 << n)) != 0 for n in range(3)]
    mid_block = ((r_i // 8) % 2 == 1) & (c_i // 8 == r_i // 8 - 1)

    def chunk_matmuls(c):
        r0 = pl.multiple_of(c * CHUNK, CHUNK)
        sl = pl.ds(r0, CHUNK)
        vc = i_ref[sl, :]
        qt_c = qt_ref[sl, :]

        q_blk, k_blk, b_blk, tot, khat = [], [], [], [], []
        for blk in range(n_sub):
            bs = pl.ds(pl.multiple_of(r0 + blk * SUB, SUB), SUB)
            q_blk.append(qf_ref[bs, :])
            k_blk.append(k_ref[bs, :])
            b_blk.append(b_ref[bs, :])
            tot.append(b_ref[pl.ds(r0 + blk * SUB + SUB - 1, 1), :])
            khat.append(k_blk[blk] * jnp.exp2(tot[blk] - b_blk[blk]))

        def span(lo, hi):
            return functools.reduce(lambda a, t: a + t, tot[lo + 1:hi], tot[lo])

        def keys_seen_from(blk):
            parts = [khat[j] * jnp.exp2(span(j + 1, blk)) if j < blk - 1 else khat[j]
                     for j in range(n_sub)]
            return jnp.concatenate(parts, axis=0)

        k_stack = jnp.concatenate([keys_seen_from(blk) for blk in range(1, n_sub)],
                                  axis=0).astype(jnp.bfloat16)
        s_all = lax.dot_general(qt_c.astype(jnp.bfloat16), k_stack, _NT,
                                preferred_element_type=jnp.float32)

        q_mid, k_mid = [], []
        for blk in range(n_sub):
            mid = b_ref[pl.ds(r0 + blk * SUB + 7, 1), :]
            q_mid += [q_blk[blk][0:8, :], q_blk[blk][8:16, :] * jnp.exp2(b_blk[blk][8:16, :] - mid)]
            k_mid += [k_blk[blk][0:8, :] * jnp.exp2(mid - b_blk[blk][0:8, :]), k_blk[blk][8:16, :]]
        s_mid = lax.dot_general(jnp.concatenate(q_mid, axis=0).astype(jnp.bfloat16),
                                jnp.concatenate(k_mid, axis=0).astype(jnp.bfloat16), _NT,
                                preferred_element_type=jnp.float32)

        upd = jnp.dot(vc.astype(jnp.float32).T.astype(jnp.bfloat16),
                      keys_seen_from(n_sub).astype(jnp.bfloat16),
                      preferred_element_type=jnp.float32)

        q_state = jnp.concatenate(
            [qt_c[0:SUB, :]] + [qt_c[blk * SUB:(blk + 1) * SUB, :] * jnp.exp2(span(0, blk))
                                for blk in range(1, n_sub)], axis=0).astype(jnp.bfloat16)
        return dict(r0=r0, sl=sl, vc=vc, q_blk=q_blk, b_blk=b_blk,
                    s_all=s_all, s_mid=s_mid, upd=upd, q_state=q_state,
                    decay=jnp.exp2(span(0, n_sub)))

    def chunk_scores(w):
        r0 = w["r0"]
        off_rows = [jnp.zeros((SUB, CHUNK), jnp.float32)]
        for blk in range(1, n_sub):
            off_rows.append(w["s_all"][blk * SUB:(blk + 1) * SUB, (blk - 1) * CHUNK:blk * CHUNK])
        s_off = jnp.concatenate(off_rows, axis=0)

        diag_rows = []
        for blk in range(n_sub):
            for half in range(SUB // 8):
                q_h = w["q_blk"][blk][half * 8:(half + 1) * 8, :]
                b_h = w["b_blk"][blk][half * 8:(half + 1) * 8, :]
                cols = []
                for s in range(half * 8, (half + 1) * 8):
                    row_s = pl.ds(r0 + blk * SUB + s, 1)
                    d = b_h - b_ref[row_s, :]
                    if s > half * 8:
                        d = jnp.minimum(d, 0.0)
                    cols.append(jnp.sum(q_h * k_ref[row_s, :] * jnp.exp2(d),
                                        axis=-1, keepdims=True))
                bit = 0
                while len(cols) > 1:
                    cols = [jnp.where(lane_bits[bit], cols[n + 1], cols[n])
                            for n in range(0, len(cols), 2)]
                    bit += 1
                diag_rows.append(cols[0])
        s_diag = jnp.concatenate(diag_rows, axis=0)

        return jnp.where(
            causal, jnp.where(off_diag, s_off, jnp.where(mid_block, w["s_mid"], s_diag)),
            0.0).astype(jnp.bfloat16)

    def group_body(g, carry):
        work = [chunk_matmuls(g * GROUP + n) for n in range(GROUP)]
        scores = [chunk_scores(w) for w in work]
        state_t = state_ref[...]
        for w, sc in zip(work, scores):
            o_c = jnp.dot(sc, w["vc"], preferred_element_type=jnp.float32)
            o_c = o_c + lax.dot_general(w["q_state"], state_t.astype(jnp.bfloat16), _NT,
                                        preferred_element_type=jnp.float32)
            oacc_ref[w["sl"], :] = o_c
            state_t = state_t * w["decay"] + w["upd"]
        state_ref[...] = state_t
        return carry

    lax.fori_loop(0, n_chunks // GROUP, group_body, 0)

    o = oacc_ref[...]
    o = o * _rms_scale(o) * nw_ref[...]
    o_ref[...] = (o * _silu(g_ref[...].astype(jnp.float32))).astype(o_ref.dtype)


def _hgrn(proj, hgrn_lb, norm_w, *, batch, seq, rows):
    n_row_blocks = seq // rows
    hb = HG_HEADS

    def col_spec(group):
        return pl.BlockSpec((rows, HG_D),
                            lambda b, h, t: (b * n_row_blocks + t, group * hb + h))

    scratch = [pltpu.VMEM((HG_D, HG_D), jnp.float32)]
    scratch += [pltpu.VMEM((rows, HG_D), jnp.float32) for _ in range(5)]
    return pl.pallas_call(
        functools.partial(_hgrn_kernel, rows=rows),
        grid=(batch, hb, n_row_blocks),
        in_specs=[col_spec(0), col_spec(1), col_spec(2), col_spec(3),
                  pl.BlockSpec((hgrn_lb.shape[0], HG_D), lambda b, h, t: (0, h)),
                  pl.BlockSpec((1, HG_D), lambda b, h, t: (0, 0))],
        out_specs=pl.BlockSpec((rows, HG_D), lambda b, h, t: (b * n_row_blocks + t, h)),
        out_shape=jax.ShapeDtypeStruct((batch * seq, HG_WIDTH), jnp.bfloat16),
        scratch_shapes=scratch,
        compiler_params=pltpu.CompilerParams(
            dimension_semantics=("parallel", "parallel", "arbitrary"),
            vmem_limit_bytes=VMEM_LIMIT),
        name="hgrn2",
    )(proj, proj, proj, proj, hgrn_lb, norm_w)


def _rope(u, cos_t, sin_t):
    return u * cos_t + pltpu.roll(u, LANE // 2, axis=1) * sin_t


def _rope_table_kernel(pos_ref, invf_ref, cos_ref, sin_ref):
    ang = pos_ref[...].astype(jnp.float32) * invf_ref[...]
    cos_ref[...] = jnp.cos(ang)
    sin_ref[...] = jnp.sin(ang)


def _rope_table(pos_row, invf_col, *, tn):
    m = pos_row.shape[1]
    nf = invf_col.shape[0]
    return pl.pallas_call(
        _rope_table_kernel,
        grid=(m // tn,),
        in_specs=[pl.BlockSpec((1, tn), lambda i: (0, i)),
                  pl.BlockSpec((nf, 1), lambda i: (0, 0))],
        out_specs=[pl.BlockSpec((nf, tn), lambda i: (0, i)),
                   pl.BlockSpec((nf, tn), lambda i: (0, i))],
        out_shape=[jax.ShapeDtypeStruct((nf, m), jnp.float32),
                   jax.ShapeDtypeStruct((nf, m), jnp.float32)],
        compiler_params=pltpu.CompilerParams(dimension_semantics=("parallel",)),
        name="rope_table",
    )(pos_row, invf_col)


def _mla_prep_kernel(cq_ref, ckv_ref, kr_ref, cos_ref, sin_ref, qnw_ref, kvnw_ref, wuq_ref,
                     wukv_ref, qhw_ref, khw_ref, q_ref, k_ref, v_ref):
    cq = cq_ref[...].astype(jnp.float32)
    ckv = ckv_ref[...].astype(jnp.float32)
    cqn = (cq * _rms_scale(cq) * qnw_ref[...]).astype(jnp.bfloat16)
    ckvn = (ckv * _rms_scale(ckv) * kvnw_ref[...]).astype(jnp.bfloat16)

    half = QK_ROPE // 2
    lane = lax.broadcasted_iota(jnp.int32, (1, LANE), 1)
    hi = (lane >= LANE // 2) & (lane < LANE // 2 + half)
    kr_raw = kr_ref[...].astype(jnp.float32)
    kr = jnp.where(lane < half, kr_raw, 0.0) + jnp.where(
        hi, pltpu.roll(kr_raw, LANE // 2 - half, axis=1), 0.0)

    rows = cos_ref.shape[1]
    zpad = jnp.zeros((LANE - half, rows), jnp.float32)
    cos_lo = jnp.concatenate([cos_ref[...], zpad], axis=0).T
    sin_lo = jnp.concatenate([sin_ref[...], zpad], axis=0).T
    cos_t = cos_lo + pltpu.roll(cos_lo, LANE // 2, axis=1)
    sin_t = pltpu.roll(sin_lo, LANE // 2, axis=1) - sin_lo

    qhw = qhw_ref[...]
    khw = khw_ref[...]
    kr_ss = jnp.sum(kr * kr, axis=-1, keepdims=True)
    kr_rot = _rope(kr * khw[:, LANE:QK_PAD], cos_t, sin_t)
    for h in range(MLA_HEADS):
        c0 = h * QK_PAD
        q = jnp.dot(cqn, wuq_ref[:, c0:c0 + QK_PAD], preferred_element_type=jnp.float32)
        qn = q[:, 0:LANE]
        qr = q[:, LANE:QK_PAD]
        ss = jnp.sum(qn * qn + qr * qr, axis=-1, keepdims=True)
        inv = lax.rsqrt(ss * (1.0 / QK_DIM) + NORM_EPS) * (ATTN_SCALE * LOG2_E)
        q_ref[:, c0:c0 + LANE] = (qn * inv * qhw[:, 0:LANE]).astype(q_ref.dtype)
        q_ref[:, c0 + LANE:c0 + QK_PAD] = _rope(qr * inv * qhw[:, LANE:QK_PAD],
                                                cos_t, sin_t).astype(q_ref.dtype)

        kv = jnp.dot(ckvn, wukv_ref[:, c0:c0 + QK_PAD], preferred_element_type=jnp.float32)
        kn = kv[:, 0:LANE]
        ss = jnp.sum(kn * kn, axis=-1, keepdims=True) + kr_ss
        inv = lax.rsqrt(ss * (1.0 / QK_DIM) + NORM_EPS)
        k_ref[:, c0:c0 + LANE] = (kn * inv * khw[:, 0:LANE]).astype(k_ref.dtype)
        k_ref[:, c0 + LANE:c0 + QK_PAD] = (kr_rot * inv).astype(k_ref.dtype)
        v_ref[:, c0:c0 + V_DIM] = kv[:, LANE:QK_PAD].astype(v_ref.dtype)
        v_ref[:, c0 + V_DIM:c0 + 2 * V_DIM] = jnp.ones((v_ref.shape[0], V_DIM), v_ref.dtype)


def _mla_prep(proj, cos_tab, sin_tab, qnw, kvnw, wuq, wukv, qhw, khw, *, tm):
    m = proj.shape[0]
    full = lambda a: pl.BlockSpec(a.shape, lambda i: (0, 0))
    hg_cols = 4 * HG_WIDTH
    nf = cos_tab.shape[0]
    return pl.pallas_call(
        _mla_prep_kernel,
        grid=(m // tm,),
        in_specs=[pl.BlockSpec((tm, Q_LORA), lambda i: (i, hg_cols // Q_LORA)),
                  pl.BlockSpec((tm, KV_LORA), lambda i: (i, (hg_cols + Q_LORA) // KV_LORA)),
                  pl.BlockSpec((tm, LANE), lambda i: (i, (hg_cols + Q_LORA + KV_LORA) // LANE)),
                  pl.BlockSpec((nf, tm), lambda i: (0, i)),
                  pl.BlockSpec((nf, tm), lambda i: (0, i)),
                  full(qnw), full(kvnw), full(wuq), full(wukv), full(qhw), full(khw)],
        out_specs=[pl.BlockSpec((tm, MLA_HEADS * QK_PAD), lambda i: (i, 0)),
                   pl.BlockSpec((tm, MLA_HEADS * QK_PAD), lambda i: (i, 0)),
                   pl.BlockSpec((tm, MLA_HEADS * 2 * V_DIM), lambda i: (i, 0))],
        out_shape=[jax.ShapeDtypeStruct((m, MLA_HEADS * QK_PAD), jnp.bfloat16),
                   jax.ShapeDtypeStruct((m, MLA_HEADS * QK_PAD), jnp.bfloat16),
                   jax.ShapeDtypeStruct((m, MLA_HEADS * 2 * V_DIM), jnp.bfloat16)],
        compiler_params=pltpu.CompilerParams(
            dimension_semantics=("parallel",),
            vmem_limit_bytes=VMEM_LIMIT),
        name="mla_prep",
    )(proj, proj, proj, cos_tab, sin_tab, qnw, kvnw, wuq, wukv, qhw, khw)


def _attn_kernel(q_ref, k_ref, v_ref, o_ref, sa_ref, sb_ref, m_ref, acc_ref, *, tile, n_tiles):
    n_col = tile // LANE

    def scores(i, j, s_ref):
        qs = pl.ds(pl.multiple_of(i * tile, tile), tile)
        ks = pl.ds(pl.multiple_of(j * tile, tile), tile)
        s_ref[...] = lax.dot_general(q_ref[qs, :], k_ref[ks, :], _NT,
                                     preferred_element_type=jnp.float32)

    def start_tile():
        m_ref[...] = jnp.full_like(m_ref, -jnp.inf)
        acc_ref[...] = jnp.zeros_like(acc_ref)

    def finish_tile(i):
        acc = acc_ref[...]
        qs = pl.ds(pl.multiple_of(i * tile, tile), tile)
        o_ref[qs, :] = (acc[:, :V_DIM] / acc[:, V_DIM:]).astype(o_ref.dtype)

    def softmax_rows(s_ref, rows, n_vis, half_last):
        cols = [s_ref[rows, c * LANE:(c + 1) * LANE] for c in range(n_vis)]
        if half_last:
            first_chunk = lax.broadcasted_iota(jnp.int32, (1, LANE), 1) < CHUNK
            cols[-1] = jnp.where(first_chunk, cols[-1], -jnp.inf)
        m_prev = m_ref[rows, :]
        m_cur = functools.reduce(jnp.maximum, cols)
        m_new = jnp.maximum(m_prev, jnp.max(m_cur, axis=-1, keepdims=True))
        alpha = jnp.exp2(m_prev - m_new)
        p = [jnp.exp2(s_c - m_new) for s_c in cols]
        p += [jnp.zeros_like(p[0])] * (n_col - n_vis)
        m_ref[rows, :] = m_new
        return jnp.concatenate(p, axis=1).astype(v_ref.dtype), alpha

    def softmax_pv(j, s_ref, masked):
        ks = pl.ds(pl.multiple_of(j * tile, tile), tile)
        if masked:
            lane_chunks = LANE // CHUNK
            parts = [softmax_rows(s_ref, slice(c * CHUNK, (c + 1) * CHUNK),
                                  c // lane_chunks + 1, c % lane_chunks != lane_chunks - 1)
                     for c in range(tile // CHUNK)]
            p = jnp.concatenate([pp for pp, _ in parts], axis=0)
            alpha = jnp.concatenate([aa for _, aa in parts], axis=0)
        else:
            p, alpha = softmax_rows(s_ref, slice(None), n_col, False)
        pv = jnp.dot(p, v_ref[ks, :], preferred_element_type=jnp.float32)
        acc_ref[...] = jnp.concatenate([alpha, alpha], axis=1) * acc_ref[...] + pv

    def query_tile(i, r, first, other):
        start_tile()

        def pair(jj, carry):
            j = 2 * jj
            scores(i, j + 1, other)
            softmax_pv(j, first, False)
            scores(i, j + 2, first)
            softmax_pv(j + 1, other, False)
            return carry

        n_pairs = i // 2
        lax.fori_loop(0, n_pairs, pair, 0)
        nxt = jnp.minimum(i + 1, n_tiles - 1)
        if r % 2 == 1:
            scores(i, i, other)
            softmax_pv(i - 1, first, False)
            scores(nxt, 0, first)
            softmax_pv(i, other, True)
        else:
            scores(nxt, 0, other)
            softmax_pv(i, first, True)
        finish_tile(i)

    scores(0, 0, sa_ref)

    def four_tiles(g, carry):
        i0 = 4 * g
        query_tile(i0, 0, sa_ref, sb_ref)
        query_tile(i0 + 1, 1, sb_ref, sa_ref)
        query_tile(i0 + 2, 2, sb_ref, sa_ref)
        query_tile(i0 + 3, 3, sa_ref, sb_ref)
        return carry

    lax.fori_loop(0, n_tiles // 4, four_tiles, 0)


def _attention(q, k, v1, *, batch, seq, tile):
    n_tiles = seq // tile
    assert n_tiles % 4 == 0
    vw = 2 * V_DIM
    return pl.pallas_call(
        functools.partial(_attn_kernel, tile=tile, n_tiles=n_tiles),
        grid=(batch, MLA_HEADS),
        in_specs=[pl.BlockSpec((seq, QK_PAD), lambda b, h: (b, h)),
                  pl.BlockSpec((seq, QK_PAD), lambda b, h: (b, h)),
                  pl.BlockSpec((seq, vw), lambda b, h: (b, h))],
        out_specs=pl.BlockSpec((seq, V_DIM), lambda b, h: (b, h)),
        out_shape=jax.ShapeDtypeStruct((batch * seq, MLA_HEADS * V_DIM), jnp.bfloat16),
        scratch_shapes=[pltpu.VMEM((tile, tile), jnp.float32),
                        pltpu.VMEM((tile, tile), jnp.float32),
                        pltpu.VMEM((tile, LANE), jnp.float32),
                        pltpu.VMEM((tile, vw), jnp.float32)],
        compiler_params=pltpu.CompilerParams(
            dimension_semantics=("parallel", "parallel"),
            vmem_limit_bytes=VMEM_LIMIT),
        name="mla_attention",
    )(q, k, v1)


def _rope_lanes(a):
    half = QK_ROPE // 2
    z = jnp.zeros(a.shape[:-1] + (LANE // 2 - half,), a.dtype)
    return jnp.concatenate([a[..., :half], z, a[..., half:], z], axis=-1)


def _head_lanes(a):
    return jnp.concatenate([a[..., :QK_NOPE], _rope_lanes(a[..., QK_NOPE:])], axis=-1)


def kernel(x, positions, attn_norm_w, w_in, hgrn_lb, hgrn_norm_w, mla_q_norm_w, w_uq,
           mla_kv_norm_w, w_ukv, q_head_norm_w, k_head_norm_w, w_out, ffn_norm_w,
           w_gate_up, w_down):
    batch, seq, d = x.shape
    m = batch * seq
    bf = jnp.bfloat16
    layer = 0
    xf = x.reshape(m, d)

    hg_cols = 4 * HG_WIDTH
    w_in_l = w_in[layer]
    w_proj = jnp.pad(w_in_l, ((0, 0), (0, PROJ_COLS - w_in_l.shape[1]))).astype(bf)
    wuq = _head_lanes(w_uq[layer].reshape(Q_LORA, MLA_HEADS, QK_DIM)).reshape(
        Q_LORA, MLA_HEADS * QK_PAD).astype(bf)
    wukv = w_ukv[layer].astype(bf)
    qhw = _head_lanes(q_head_norm_w[layer])[None, :]
    khw = _head_lanes(k_head_norm_w[layer])[None, :]
    inv_freq = 1.0 / (ROPE_THETA ** (jnp.arange(0, QK_ROPE, 2, dtype=jnp.float32) / QK_ROPE))
    nw_attn = attn_norm_w[layer][None, :]

    proj = _norm_matmul(xf, nw_attn, w_proj, tm=1024, tn=PROJ_COLS // 7,
                        out_dtype=bf, name="in_proj")

    o_a = _hgrn(proj, hgrn_lb, hgrn_norm_w[layer][None, :],
                batch=batch, seq=seq, rows=1024)

    cos_tab, sin_tab = _rope_table(positions.reshape(1, m), inv_freq[:, None], tn=2048)
    q, k, v = _mla_prep(proj, cos_tab, sin_tab,
                        mla_q_norm_w[layer][None, :], mla_kv_norm_w[layer][None, :],
                        wuq, wukv, qhw, khw, tm=512)
    o_b = _attention(q, k, v, batch=batch, seq=seq, tile=512)

    x1 = _res_matmul([o_a, o_b], w_out[layer].astype(bf), xf, tm=1024, tn=512, name="out_proj")

    act = _norm_swiglu(x1, ffn_norm_w[layer][None, :], w_gate_up[layer].astype(bf),
                       tm=1024, tn=512)
    x2 = _res_matmul([act], w_down[layer].astype(bf), x1, tm=1024, tn=512, name="ffn_down")
    return x2.reshape(batch, seq, d)
```

```python
import functools

import jax
import jax.numpy as jnp
import numpy as np
from jax import lax
from jax.experimental import pallas as pl
from jax.experimental.pallas import tpu as pltpu

D_MODEL = 2048
CHUNK = 64
SUB = 16
GROUP = 16
HG_WIDTH = 1024
HG_HEADS = 8
HG_D = 128
MLA_HEADS = 8
V_DIM = 128
QK_NOPE = 128
QK_ROPE = 64
QK_DIM = QK_NOPE + QK_ROPE
QK_PAD = 256
Q_LORA = 512
KV_LORA = 512
ROPE_THETA = 10000.0
ATTN_SCALE = QK_DIM ** -0.5
LOG2_E = 1.4426950408889634
D_FF = 5632
NORM_EPS = 1e-6
LANE = 128
PROJ_COLS = 5376

VMEM_LIMIT = 56 * 1024 * 1024

_NT = (((1,), (1,)), ((), ()))


def _silu(x):
    return x * (1.0 / (1.0 + jnp.exp(-x)))


def _rms_scale(x):
    return lax.rsqrt(jnp.mean(x * x, axis=-1, keepdims=True) + NORM_EPS)


def _norm_matmul_kernel(x_ref, nw_ref, w_ref, o_ref, h_ref):
    @pl.when(pl.program_id(1) == 0)
    def _():
        x = x_ref[...]
        h_ref[...] = (x * _rms_scale(x) * nw_ref[...]).astype(h_ref.dtype)

    o_ref[...] = jnp.dot(h_ref[...], w_ref[...],
                         preferred_element_type=jnp.float32).astype(o_ref.dtype)


def _norm_matmul(x, nw, w, *, tm, tn, out_dtype, name):
    m, k = x.shape
    n = w.shape[1]
    return pl.pallas_call(
        _norm_matmul_kernel,
        grid=(m // tm, n // tn),
        in_specs=[pl.BlockSpec((tm, k), lambda i, j: (i, 0)),
                  pl.BlockSpec((1, k), lambda i, j: (0, 0)),
                  pl.BlockSpec((k, tn), lambda i, j: (0, j))],
        out_specs=pl.BlockSpec((tm, tn), lambda i, j: (i, j)),
        out_shape=jax.ShapeDtypeStruct((m, n), out_dtype),
        scratch_shapes=[pltpu.VMEM((tm, k), jnp.bfloat16)],
        compiler_params=pltpu.CompilerParams(
            dimension_semantics=("parallel", "arbitrary"),
            vmem_limit_bytes=VMEM_LIMIT),
        name=name,
    )(x, nw, w)


def _norm_swiglu_kernel(x_ref, nw_ref, wg_ref, wu_ref, o_ref, h_ref):
    @pl.when(pl.program_id(1) == 0)
    def _():
        x = x_ref[...]
        h_ref[...] = (x * _rms_scale(x) * nw_ref[...]).astype(h_ref.dtype)

    h = h_ref[...]
    g = jnp.dot(h, wg_ref[...], preferred_element_type=jnp.float32)
    u = jnp.dot(h, wu_ref[...], preferred_element_type=jnp.float32)
    o_ref[...] = (_silu(g) * u).astype(o_ref.dtype)


def _norm_swiglu(x, nw, w_gate_up, *, tm, tn):
    m, k = x.shape
    n = w_gate_up.shape[1] // 2
    up_off = n // tn
    return pl.pallas_call(
        _norm_swiglu_kernel,
        grid=(m // tm, n // tn),
        in_specs=[pl.BlockSpec((tm, k), lambda i, j: (i, 0)),
                  pl.BlockSpec((1, k), lambda i, j: (0, 0)),
                  pl.BlockSpec((k, tn), lambda i, j: (0, j)),
                  pl.BlockSpec((k, tn), lambda i, j: (0, j + up_off))],
        out_specs=pl.BlockSpec((tm, tn), lambda i, j: (i, j)),
        out_shape=jax.ShapeDtypeStruct((m, n), jnp.bfloat16),
        scratch_shapes=[pltpu.VMEM((tm, k), jnp.bfloat16)],
        compiler_params=pltpu.CompilerParams(
            dimension_semantics=("parallel", "arbitrary"),
            vmem_limit_bytes=VMEM_LIMIT),
        name="ffn_up",
    )(x, nw, w_gate_up, w_gate_up)


def _res_matmul_kernel(*refs, n_a):
    a_refs = refs[:n_a]
    w_refs = refs[n_a:2 * n_a]
    r_ref = refs[2 * n_a]
    o_ref = refs[2 * n_a + 1]
    acc = r_ref[...]
    for a_ref, w_ref in zip(a_refs, w_refs):
        acc = acc + jnp.dot(a_ref[...], w_ref[...], preferred_element_type=jnp.float32)
    o_ref[...] = acc


def _res_matmul(a_list, w, res, *, tm, tn, name):
    m, n = res.shape
    n_a = len(a_list)
    ka = a_list[0].shape[1]
    in_specs = [pl.BlockSpec((tm, ka), lambda i, j: (i, 0)) for _ in a_list]
    in_specs += [pl.BlockSpec((ka, tn), functools.partial(lambda i, j, kb: (kb, j), kb=kb))
                 for kb in range(n_a)]
    in_specs += [pl.BlockSpec((tm, tn), lambda i, j: (i, j))]
    return pl.pallas_call(
        functools.partial(_res_matmul_kernel, n_a=n_a),
        grid=(m // tm, n // tn),
        in_specs=in_specs,
        out_specs=pl.BlockSpec((tm, tn), lambda i, j: (i, j)),
        out_shape=jax.ShapeDtypeStruct((m, n), jnp.float32),
        compiler_params=pltpu.CompilerParams(
            dimension_semantics=("parallel", "arbitrary"),
            vmem_limit_bytes=VMEM_LIMIT),
        name=name,
    )(*a_list, *([w] * n_a), res)


def _segmented_cumsum(x, row, seg):
    pos = row % seg
    shift = 1
    while shift < seg:
        x = x + jnp.where(pos >= shift, pltpu.roll(x, shift, axis=0), 0.0)
        shift *= 2
    return x


def _hgrn_kernel(q_ref, f_ref, i_ref, g_ref, lb_ref, nw_ref, o_ref,
                 state_ref, qf_ref, qt_ref, k_ref, b_ref, oacc_ref, *, rows):
    n_chunks = rows // CHUNK
    n_sub = CHUNK // SUB

    @pl.when(pl.program_id(2) == 0)
    def _():
        state_ref[...] = jnp.zeros_like(state_ref)

    lbr = lb_ref[...]
    lbe = jnp.exp(lbr - jnp.max(lbr, axis=0, keepdims=True))
    lb = lbe[0:1, :] / jnp.sum(lbe, axis=0, keepdims=True)

    qf = _silu(q_ref[...].astype(jnp.float32))
    f = lb + (1.0 - lb) * (1.0 / (1.0 + jnp.exp(-f_ref[...].astype(jnp.float32))))
    row = lax.broadcasted_iota(jnp.int32, (rows, HG_D), 0)
    b = _segmented_cumsum(jnp.log2(f), row, SUB)
    qf_ref[...] = qf
    qt_ref[...] = qf * jnp.exp2(b)
    k_ref[...] = 1.0 - f
    b_ref[...] = b

    r_i = lax.broadcasted_iota(jnp.int32, (CHUNK, CHUNK), 0)
    c_i = lax.broadcasted_iota(jnp.int32, (CHUNK, CHUNK), 1)
    causal = c_i <= r_i
    off_diag = c_i < (r_i // SUB) * SUB
    lane8 = lax.broadcasted_iota(jnp.int32, (8, CHUNK), 1)
    lane_bits = [(lane8 & (1 << n)) != 0 for n in range(3)]
    mid_block = ((r_i // 8) % 2 == 1) & (c_i // 8 == r_i // 8 - 1)

    def chunk_matmuls(c):
        r0 = pl.multiple_of(c * CHUNK, CHUNK)
        sl = pl.ds(r0, CHUNK)
        vc = i_ref[sl, :]
        qt_c = qt_ref[sl, :]

        q_blk, k_blk, b_blk, tot, khat = [], [], [], [], []
        for blk in range(n_sub):
            bs = pl.ds(pl.multiple_of(r0 + blk * SUB, SUB), SUB)
            q_blk.append(qf_ref[bs, :])
            k_blk.append(k_ref[bs, :])
            b_blk.append(b_ref[bs, :])
            tot.append(b_ref[pl.ds(r0 + blk * SUB + SUB - 1, 1), :])
            khat.append(k_blk[blk] * jnp.exp2(tot[blk] - b_blk[blk]))

        def span(lo, hi):
            return functools.reduce(lambda a, t: a + t, tot[lo + 1:hi], tot[lo])

        def keys_seen_from(blk):
            parts = [khat[j] * jnp.exp2(span(j + 1, blk)) if j < blk - 1 else khat[j]
                     for j in range(n_sub)]
            return jnp.concatenate(parts, axis=0)

        k_stack = jnp.concatenate([keys_seen_from(blk) for blk in range(1, n_sub)],
                                  axis=0).astype(jnp.bfloat16)
        s_all = lax.dot_general(qt_c.astype(jnp.bfloat16), k_stack, _NT,
                                preferred_element_type=jnp.float32)

        q_mid, k_mid = [], []
        for blk in range(n_sub):
            mid = b_ref[pl.ds(r0 + blk * SUB + 7, 1), :]
            q_mid += [q_blk[blk][0:8, :], q_blk[blk][8:16, :] * jnp.exp2(b_blk[blk][8:16, :] - mid)]
            k_mid += [k_blk[blk][0:8, :] * jnp.exp2(mid - b_blk[blk][0:8, :]), k_blk[blk][8:16, :]]
        s_mid = lax.dot_general(jnp.concatenate(q_mid, axis=0).astype(jnp.bfloat16),
                                jnp.concatenate(k_mid, axis=0).astype(jnp.bfloat16), _NT,
                                preferred_element_type=jnp.float32)

        upd = jnp.dot(vc.astype(jnp.float32).T.astype(jnp.bfloat16),
                      keys_seen_from(n_sub).astype(jnp.bfloat16),
                      preferred_element_type=jnp.float32)

        q_state = jnp.concatenate(
            [qt_c[0:SUB, :]] + [qt_c[blk * SUB:(blk + 1) * SUB, :] * jnp.exp2(span(0, blk))
                                for blk in range(1, n_sub)], axis=0).astype(jnp.bfloat16)
        return dict(r0=r0, sl=sl, vc=vc, q_blk=q_blk, b_blk=b_blk,
                    s_all=s_all, s_mid=s_mid, upd=upd, q_state=q_state,
                    decay=jnp.exp2(span(0, n_sub)))

    def chunk_scores(w):
        r0 = w["r0"]
        off_rows = [jnp.zeros((SUB, CHUNK), jnp.float32)]
        for blk in range(1, n_sub):
            off_rows.append(w["s_all"][blk * SUB:(blk + 1) * SUB, (blk - 1) * CHUNK:blk * CHUNK])
        s_off = jnp.concatenate(off_rows, axis=0)

        diag_rows = []
        for blk in range(n_sub):
            for half in range(SUB // 8):
                q_h = w["q_blk"][blk][half * 8:(half + 1) * 8, :]
                b_h = w["b_blk"][blk][half * 8:(half + 1) * 8, :]
                cols = []
                for s in range(half * 8, (half + 1) * 8):
                    row_s = pl.ds(r0 + blk * SUB + s, 1)
                    d = b_h - b_ref[row_s, :]
                    if s > half * 8:
                        d = jnp.minimum(d, 0.0)
                    cols.append(jnp.sum(q_h * k_ref[row_s, :] * jnp.exp2(d),
                                        axis=-1, keepdims=True))
                bit = 0
                while len(cols) > 1:
                    cols = [jnp.where(lane_bits[bit], cols[n + 1], cols[n])
                            for n in range(0, len(cols), 2)]
                    bit += 1
                diag_rows.append(cols[0])
        s_diag = jnp.concatenate(diag_rows, axis=0)

        return jnp.where(
            causal, jnp.where(off_diag, s_off, jnp.where(mid_block, w["s_mid"], s_diag)),
            0.0).astype(jnp.bfloat16)

    def group_body(g, carry):
        work = [chunk_matmuls(g * GROUP + n) for n in range(GROUP)]
        scores = [chunk_scores(w) for w in work]
        state_t = state_ref[...]
        for w, sc in zip(work, scores):
            o_c = jnp.dot(sc, w["vc"], preferred_element_type=jnp.float32)
            o_c = o_c + lax.dot_general(w["q_state"], state_t.astype(jnp.bfloat16), _NT,
                                        preferred_element_type=jnp.float32)
            oacc_ref[w["sl"], :] = o_c
            state_t = state_t * w["decay"] + w["upd"]
        state_ref[...] = state_t
        return carry

    lax.fori_loop(0, n_chunks // GROUP, group_body, 0)

    o = oacc_ref[...]
    o = o * _rms_scale(o) * nw_ref[...]
    o_ref[...] = (o * _silu(g_ref[...].astype(jnp.float32))).astype(o_ref.dtype)


def _hgrn(proj, hgrn_lb, norm_w, *, batch, seq, rows):
    n_row_blocks = seq // rows
    hb = HG_HEADS

    def col_spec(group):
        return pl.BlockSpec((rows, HG_D),
                            lambda b, h, t: (b * n_row_blocks + t, group * hb + h))

    scratch = [pltpu.VMEM((HG_D, HG_D), jnp.float32)]
    scratch += [pltpu.VMEM((rows, HG_D), jnp.float32) for _ in range(5)]
    return pl.pallas_call(
        functools.partial(_hgrn_kernel, rows=rows),
        grid=(batch, hb, n_row_blocks),
        in_specs=[col_spec(0), col_spec(1), col_spec(2), col_spec(3),
                  pl.BlockSpec((hgrn_lb.shape[0], HG_D), lambda b, h, t: (0, h)),
                  pl.BlockSpec((1, HG_D), lambda b, h, t: (0, 0))],
        out_specs=pl.BlockSpec((rows, HG_D), lambda b, h, t: (b * n_row_blocks + t, h)),
        out_shape=jax.ShapeDtypeStruct((batch * seq, HG_WIDTH), jnp.bfloat16),
        scratch_shapes=scratch,
        compiler_params=pltpu.CompilerParams(
            dimension_semantics=("parallel", "parallel", "arbitrary"),
            vmem_limit_bytes=VMEM_LIMIT),
        name="hgrn2",
    )(proj, proj, proj, proj, hgrn_lb, norm_w)


def _rope(u, cos_t, sin_t):
    return u * cos_t + pltpu.roll(u, LANE // 2, axis=1) * sin_t


def _rope_table_kernel(pos_ref, invf_ref, cos_ref, sin_ref):
    ang = pos_ref[...].astype(jnp.float32) * invf_ref[...]
    cos_ref[...] = jnp.cos(ang)
    sin_ref[...] = jnp.sin(ang)


def _rope_table(pos_row, invf_col, *, tn):
    m = pos_row.shape[1]
    nf = invf_col.shape[0]
    return pl.pallas_call(
        _rope_table_kernel,
        grid=(m // tn,),
        in_specs=[pl.BlockSpec((1, tn), lambda i: (0, i)),
                  pl.BlockSpec((nf, 1), lambda i: (0, 0))],
        out_specs=[pl.BlockSpec((nf, tn), lambda i: (0, i)),
                   pl.BlockSpec((nf, tn), lambda i: (0, i))],
        out_shape=[jax.ShapeDtypeStruct((nf, m), jnp.float32),
                   jax.ShapeDtypeStruct((nf, m), jnp.float32)],
        compiler_params=pltpu.CompilerParams(dimension_semantics=("parallel",)),
        name="rope_table",
    )(pos_row, invf_col)


def _mla_prep_kernel(cq_ref, ckv_ref, kr_ref, cos_ref, sin_ref, qnw_ref, kvnw_ref, wuq_ref,
                     wukv_ref, qhw_ref, khw_ref, q_ref, k_ref, v_ref):
    cq = cq_ref[...].astype(jnp.float32)
    ckv = ckv_ref[...].astype(jnp.float32)
    cqn = (cq * _rms_scale(cq) * qnw_ref[...]).astype(jnp.bfloat16)
    ckvn = (ckv * _rms_scale(ckv) * kvnw_ref[...]).astype(jnp.bfloat16)

    half = QK_ROPE // 2
    lane = lax.broadcasted_iota(jnp.int32, (1, LANE), 1)
    hi = (lane >= LANE // 2) & (lane < LANE // 2 + half)
    kr_raw = kr_ref[...].astype(jnp.float32)
    kr = jnp.where(lane < half, kr_raw, 0.0) + jnp.where(
        hi, pltpu.roll(kr_raw, LANE // 2 - half, axis=1), 0.0)

    rows = cos_ref.shape[1]
    zpad = jnp.zeros((LANE - half, rows), jnp.float32)
    cos_lo = jnp.concatenate([cos_ref[...], zpad], axis=0).T
    sin_lo = jnp.concatenate([sin_ref[...], zpad], axis=0).T
    cos_t = cos_lo + pltpu.roll(cos_lo, LANE // 2, axis=1)
    sin_t = pltpu.roll(sin_lo, LANE // 2, axis=1) - sin_lo

    qhw = qhw_ref[...]
    khw = khw_ref[...]
    kr_ss = jnp.sum(kr * kr, axis=-1, keepdims=True)
    kr_rot = _rope(kr * khw[:, LANE:QK_PAD], cos_t, sin_t)
    for h in range(MLA_HEADS):
        c0 = h * QK_PAD
        q = jnp.dot(cqn, wuq_ref[:, c0:c0 + QK_PAD], preferred_element_type=jnp.float32)
        qn = q[:, 0:LANE]
        qr = q[:, LANE:QK_PAD]
        ss = jnp.sum(qn * qn + qr * qr, axis=-1, keepdims=True)
        inv = lax.rsqrt(ss * (1.0 / QK_DIM) + NORM_EPS) * (ATTN_SCALE * LOG2_E)
        q_ref[:, c0:c0 + LANE] = (qn * inv * qhw[:, 0:LANE]).astype(q_ref.dtype)
        q_ref[:, c0 + LANE:c0 + QK_PAD] = _rope(qr * inv * qhw[:, LANE:QK_PAD],
                                                cos_t, sin_t).astype(q_ref.dtype)

        kv = jnp.dot(ckvn, wukv_ref[:, c0:c0 + QK_PAD], preferred_element_type=jnp.float32)
        kn = kv[:, 0:LANE]
        ss = jnp.sum(kn * kn, axis=-1, keepdims=True) + kr_ss
        inv = lax.rsqrt(ss * (1.0 / QK_DIM) + NORM_EPS)
        k_ref[:, c0:c0 + LANE] = (kn * inv * khw[:, 0:LANE]).astype(k_ref.dtype)
        k_ref[:, c0 + LANE:c0 + QK_PAD] = (kr_rot * inv).astype(k_ref.dtype)
        v_ref[:, c0:c0 + V_DIM] = kv[:, LANE:QK_PAD].astype(v_ref.dtype)
        v_ref[:, c0 + V_DIM:c0 + 2 * V_DIM] = jnp.ones((v_ref.shape[0], V_DIM), v_ref.dtype)


def _mla_prep(proj, cos_tab, sin_tab, qnw, kvnw, wuq, wukv, qhw, khw, *, tm):
    m = proj.shape[0]
    full = lambda a: pl.BlockSpec(a.shape, lambda i: (0, 0))
    hg_cols = 4 * HG_WIDTH
    nf = cos_tab.shape[0]
    return pl.pallas_call(
        _mla_prep_kernel,
        grid=(m // tm,),
        in_specs=[pl.BlockSpec((tm, Q_LORA), lambda i: (i, hg_cols // Q_LORA)),
                  pl.BlockSpec((tm, KV_LORA), lambda i: (i, (hg_cols + Q_LORA) // KV_LORA)),
                  pl.BlockSpec((tm, LANE), lambda i: (i, (hg_cols + Q_LORA + KV_LORA) // LANE)),
                  pl.BlockSpec((nf, tm), lambda i: (0, i)),
                  pl.BlockSpec((nf, tm), lambda i: (0, i)),
                  full(qnw), full(kvnw), full(wuq), full(wukv), full(qhw), full(khw)],
        out_specs=[pl.BlockSpec((tm, MLA_HEADS * QK_PAD), lambda i: (i, 0)),
                   pl.BlockSpec((tm, MLA_HEADS * QK_PAD), lambda i: (i, 0)),
                   pl.BlockSpec((tm, MLA_HEADS * 2 * V_DIM), lambda i: (i, 0))],
        out_shape=[jax.ShapeDtypeStruct((m, MLA_HEADS * QK_PAD), jnp.bfloat16),
                   jax.ShapeDtypeStruct((m, MLA_HEADS * QK_PAD), jnp.bfloat16),
                   jax.ShapeDtypeStruct((m, MLA_HEADS * 2 * V_DIM), jnp.bfloat16)],
        compiler_params=pltpu.CompilerParams(
            dimension_semantics=("parallel",),
            vmem_limit_bytes=VMEM_LIMIT),
        name="mla_prep",
    )(proj, proj, proj, cos_tab, sin_tab, qnw, kvnw, wuq, wukv, qhw, khw)


def _attn_kernel(q_ref, k_ref, v_ref, o_ref, sa_ref, sb_ref, m_ref, acc_ref, *, tile, n_tiles):
    n_col = tile // LANE
    half = tile // 2

    def scores(i, j, s_ref):
        qs = pl.ds(pl.multiple_of(i * tile, tile), tile)
        ks = pl.ds(pl.multiple_of(j * tile, tile), tile)
        s_ref[...] = lax.dot_general(q_ref[qs, :], k_ref[ks, :], _NT,
                                     preferred_element_type=jnp.float32)

    def start_tile():
        m_ref[...] = jnp.full_like(m_ref, -jnp.inf)
        acc_ref[...] = jnp.zeros_like(acc_ref)

    def finish_tile(i):
        acc = acc_ref[...]
        qs = pl.ds(pl.multiple_of(i * tile, tile), tile)
        o_ref[qs, :] = (acc[:, :V_DIM] / acc[:, V_DIM:]).astype(o_ref.dtype)

    def diag_scores(i, s_ref):
        for rows, n_keys in ((slice(0, half), half), (slice(half, tile), tile)):
            qs = pl.ds(pl.multiple_of(i * tile, tile) + rows.start, half)
            ks = pl.ds(pl.multiple_of(i * tile, tile), n_keys)
            s_ref[rows, 0:n_keys] = lax.dot_general(q_ref[qs, :], k_ref[ks, :], _NT,
                                                    preferred_element_type=jnp.float32)

    def softmax_rows(s_ref, rows, n_vis, half_last, n_out):
        cols = [s_ref[rows, c * LANE:(c + 1) * LANE] for c in range(n_vis)]
        if half_last:
            first_chunk = lax.broadcasted_iota(jnp.int32, (1, LANE), 1) < CHUNK
            cols[-1] = jnp.where(first_chunk, cols[-1], -jnp.inf)
        m_prev = m_ref[rows, :]
        m_cur = functools.reduce(jnp.maximum, cols)
        m_new = jnp.maximum(m_prev, jnp.max(m_cur, axis=-1, keepdims=True))
        alpha = jnp.exp2(m_prev - m_new)
        p = [jnp.exp2(s_c - m_new) for s_c in cols]
        p += [jnp.zeros_like(p[0])] * (n_out - n_vis)
        m_ref[rows, :] = m_new
        return jnp.concatenate(p, axis=1).astype(v_ref.dtype), alpha

    def accumulate(rows, alpha, pv):
        acc_ref[rows, :] = jnp.concatenate([alpha, alpha], axis=1) * acc_ref[rows, :] + pv

    def softmax_pv(j, s_ref):
        ks = pl.ds(pl.multiple_of(j * tile, tile), tile)
        p, alpha = softmax_rows(s_ref, slice(None), n_col, False, n_col)
        accumulate(slice(None), alpha,
                   jnp.dot(p, v_ref[ks, :], preferred_element_type=jnp.float32))

    def diag_softmax_pv(i, s_ref):
        lane_chunks = LANE // CHUNK
        for rows, n_keys in ((slice(0, half), half), (slice(half, tile), tile)):
            parts = [softmax_rows(s_ref, slice(c * CHUNK, (c + 1) * CHUNK),
                                  c // lane_chunks + 1, c % lane_chunks != lane_chunks - 1,
                                  n_keys // LANE)
                     for c in range(rows.start // CHUNK, rows.stop // CHUNK)]
            p = jnp.concatenate([pp for pp, _ in parts], axis=0)
            alpha = jnp.concatenate([aa for _, aa in parts], axis=0)
            ks = pl.ds(pl.multiple_of(i * tile, tile), n_keys)
            accumulate(rows, alpha, jnp.dot(p, v_ref[ks, :], preferred_element_type=jnp.float32))

    def query_tile(i, r, first, other):
        start_tile()

        def pair(jj, carry):
            j = 2 * jj
            scores(i, j + 1, other)
            softmax_pv(j, first)
            scores(i, j + 2, first)
            softmax_pv(j + 1, other)
            return carry

        n_pairs = i // 2
        lax.fori_loop(0, n_pairs, pair, 0)
        nxt = jnp.minimum(i + 1, n_tiles - 1)
        if r % 2 == 1:
            diag_scores(i, other)
            softmax_pv(i - 1, first)
            scores(nxt, 0, first)
            diag_softmax_pv(i, other)
        else:
            scores(nxt, 0, other)
            diag_softmax_pv(i, first)
        finish_tile(i)

    diag_scores(0, sa_ref)

    def four_tiles(g, carry):
        i0 = 4 * g
        query_tile(i0, 0, sa_ref, sb_ref)
        query_tile(i0 + 1, 1, sb_ref, sa_ref)
        query_tile(i0 + 2, 2, sb_ref, sa_ref)
        query_tile(i0 + 3, 3, sa_ref, sb_ref)
        return carry

    lax.fori_loop(0, n_tiles // 4, four_tiles, 0)


def _attention(q, k, v1, *, batch, seq, tile):
    n_tiles = seq // tile
    assert n_tiles % 4 == 0
    vw = 2 * V_DIM
    return pl.pallas_call(
        functools.partial(_attn_kernel, tile=tile, n_tiles=n_tiles),
        grid=(batch, MLA_HEADS),
        in_specs=[pl.BlockSpec((seq, QK_PAD), lambda b, h: (b, h)),
                  pl.BlockSpec((seq, QK_PAD), lambda b, h: (b, h)),
                  pl.BlockSpec((seq, vw), lambda b, h: (b, h))],
        out_specs=pl.BlockSpec((seq, V_DIM), lambda b, h: (b, h)),
        out_shape=jax.ShapeDtypeStruct((batch * seq, MLA_HEADS * V_DIM), jnp.bfloat16),
        scratch_shapes=[pltpu.VMEM((tile, tile), jnp.float32),
                        pltpu.VMEM((tile, tile), jnp.float32),
                        pltpu.VMEM((tile, LANE), jnp.float32),
                        pltpu.VMEM((tile, vw), jnp.float32)],
        compiler_params=pltpu.CompilerParams(
            dimension_semantics=("parallel", "parallel"),
            vmem_limit_bytes=VMEM_LIMIT),
        name="mla_attention",
    )(q, k, v1)


def _rope_lanes(a):
    half = QK_ROPE // 2
    z = jnp.zeros(a.shape[:-1] + (LANE // 2 - half,), a.dtype)
    return jnp.concatenate([a[..., :half], z, a[..., half:], z], axis=-1)


def _head_lanes(a):
    return jnp.concatenate([a[..., :QK_NOPE], _rope_lanes(a[..., QK_NOPE:])], axis=-1)


def kernel(x, positions, attn_norm_w, w_in, hgrn_lb, hgrn_norm_w, mla_q_norm_w, w_uq,
           mla_kv_norm_w, w_ukv, q_head_norm_w, k_head_norm_w, w_out, ffn_norm_w,
           w_gate_up, w_down):
    batch, seq, d = x.shape
    m = batch * seq
    bf = jnp.bfloat16
    layer = 0
    xf = x.reshape(m, d)

    hg_cols = 4 * HG_WIDTH
    w_in_l = w_in[layer]
    w_proj = jnp.pad(w_in_l, ((0, 0), (0, PROJ_COLS - w_in_l.shape[1]))).astype(bf)
    wuq = _head_lanes(w_uq[layer].reshape(Q_LORA, MLA_HEADS, QK_DIM)).reshape(
        Q_LORA, MLA_HEADS * QK_PAD).astype(bf)
    wukv = w_ukv[layer].astype(bf)
    qhw = _head_lanes(q_head_norm_w[layer])[None, :]
    khw = _head_lanes(k_head_norm_w[layer])[None, :]
    inv_freq = 1.0 / (ROPE_THETA ** (jnp.arange(0, QK_ROPE, 2, dtype=jnp.float32) / QK_ROPE))
    nw_attn = attn_norm_w[layer][None, :]

    proj = _norm_matmul(xf, nw_attn, w_proj, tm=1024, tn=PROJ_COLS // 7,
                        out_dtype=bf, name="in_proj")

    o_a = _hgrn(proj, hgrn_lb, hgrn_norm_w[layer][None, :],
                batch=batch, seq=seq, rows=1024)

    cos_tab, sin_tab = _rope_table(positions.reshape(1, m), inv_freq[:, None], tn=2048)
    q, k, v = _mla_prep(proj, cos_tab, sin_tab,
                        mla_q_norm_w[layer][None, :], mla_kv_norm_w[layer][None, :],
                        wuq, wukv, qhw, khw, tm=512)
    o_b = _attention(q, k, v, batch=batch, seq=seq, tile=1024)

    x1 = _res_matmul([o_a, o_b], w_out[layer].astype(bf), xf, tm=1024, tn=512, name="out_proj")

    act = _norm_swiglu(x1, ffn_norm_w[layer][None, :], w_gate_up[layer].astype(bf),
                       tm=1024, tn=512)
    x2 = _res_matmul([act], w_down[layer].astype(bf), x1, tm=1024, tn=512, name="ffn_down")
    return x2.reshape(batch, seq, d)
```

```python
import functools

import jax
import jax.numpy as jnp
import numpy as np
from jax import lax
from jax.experimental import pallas as pl
from jax.experimental.pallas import tpu as pltpu

D_MODEL = 2048
CHUNK = 64
SUB = 16
GROUP = 8
HG_WIDTH = 1024
HG_HEADS = 8
HG_D = 128
MLA_HEADS = 8
V_DIM = 128
QK_NOPE = 128
QK_ROPE = 64
QK_DIM = QK_NOPE + QK_ROPE
QK_PAD = 256
Q_LORA = 512
KV_LORA = 512
ROPE_THETA = 10000.0
ATTN_SCALE = QK_DIM ** -0.5
LOG2_E = 1.4426950408889634
D_FF = 5632
NORM_EPS = 1e-6
LANE = 128
PROJ_COLS = 5376

VMEM_LIMIT = 56 * 1024 * 1024

_NT = (((1,), (1,)), ((), ()))


def _silu(x):
    return x * (1.0 / (1.0 + jnp.exp(-x)))


def _rms_scale(x):
    return lax.rsqrt(jnp.mean(x * x, axis=-1, keepdims=True) + NORM_EPS)


def _norm_matmul_kernel(x_ref, nw_ref, w_ref, o_ref, h_ref):
    @pl.when(pl.program_id(1) == 0)
    def _():
        x = x_ref[...]
        h_ref[...] = (x * _rms_scale(x) * nw_ref[...]).astype(h_ref.dtype)

    o_ref[...] = jnp.dot(h_ref[...], w_ref[...],
                         preferred_element_type=jnp.float32).astype(o_ref.dtype)


def _norm_matmul(x, nw, w, *, tm, tn, out_dtype, name):
    m, k = x.shape
    n = w.shape[1]
    return pl.pallas_call(
        _norm_matmul_kernel,
        grid=(m // tm, n // tn),
        in_specs=[pl.BlockSpec((tm, k), lambda i, j: (i, 0)),
                  pl.BlockSpec((1, k), lambda i, j: (0, 0)),
                  pl.BlockSpec((k, tn), lambda i, j: (0, j))],
        out_specs=pl.BlockSpec((tm, tn), lambda i, j: (i, j)),
        out_shape=jax.ShapeDtypeStruct((m, n), out_dtype),
        scratch_shapes=[pltpu.VMEM((tm, k), jnp.bfloat16)],
        compiler_params=pltpu.CompilerParams(
            dimension_semantics=("parallel", "arbitrary"),
            vmem_limit_bytes=VMEM_LIMIT),
        name=name,
    )(x, nw, w)


def _norm_swiglu_kernel(x_ref, nw_ref, wg_ref, wu_ref, o_ref, h_ref):
    @pl.when(pl.program_id(1) == 0)
    def _():
        x = x_ref[...]
        h_ref[...] = (x * _rms_scale(x) * nw_ref[...]).astype(h_ref.dtype)

    h = h_ref[...]
    g = jnp.dot(h, wg_ref[...].astype(h.dtype), preferred_element_type=jnp.float32)
    u = jnp.dot(h, wu_ref[...].astype(h.dtype), preferred_element_type=jnp.float32)
    o_ref[...] = (_silu(g) * u).astype(o_ref.dtype)


def _norm_swiglu(x, nw, w_gate_up, *, tm, tn):
    m, k = x.shape
    n = w_gate_up.shape[1] // 2
    up_off = n // tn
    return pl.pallas_call(
        _norm_swiglu_kernel,
        grid=(m // tm, n // tn),
        in_specs=[pl.BlockSpec((tm, k), lambda i, j: (i, 0)),
                  pl.BlockSpec((1, k), lambda i, j: (0, 0)),
                  pl.BlockSpec((k, tn), lambda i, j: (0, j)),
                  pl.BlockSpec((k, tn), lambda i, j: (0, j + up_off))],
        out_specs=pl.BlockSpec((tm, tn), lambda i, j: (i, j)),
        out_shape=jax.ShapeDtypeStruct((m, n), jnp.bfloat16),
        scratch_shapes=[pltpu.VMEM((tm, k), jnp.bfloat16)],
        compiler_params=pltpu.CompilerParams(
            dimension_semantics=("parallel", "arbitrary"),
            vmem_limit_bytes=VMEM_LIMIT),
        name="ffn_up",
    )(x, nw, w_gate_up, w_gate_up)


def _res_matmul_kernel(*refs, n_a):
    a_refs = refs[:n_a]
    w_refs = refs[n_a:2 * n_a]
    r_ref = refs[2 * n_a]
    o_ref = refs[2 * n_a + 1]
    acc = r_ref[...]
    for a_ref, w_ref in zip(a_refs, w_refs):
        acc = acc + jnp.dot(a_ref[...], w_ref[...], preferred_element_type=jnp.float32)
    o_ref[...] = acc


def _res_matmul(a_list, w, res, *, tm, tn, name):
    m, n = res.shape
    n_a = len(a_list)
    ka = a_list[0].shape[1]
    in_specs = [pl.BlockSpec((tm, ka), lambda i, j: (i, 0)) for _ in a_list]
    in_specs += [pl.BlockSpec((ka, tn), functools.partial(lambda i, j, kb: (kb, j), kb=kb))
                 for kb in range(n_a)]
    in_specs += [pl.BlockSpec((tm, tn), lambda i, j: (i, j))]
    return pl.pallas_call(
        functools.partial(_res_matmul_kernel, n_a=n_a),
        grid=(m // tm, n // tn),
        in_specs=in_specs,
        out_specs=pl.BlockSpec((tm, tn), lambda i, j: (i, j)),
        out_shape=jax.ShapeDtypeStruct((m, n), jnp.float32),
        compiler_params=pltpu.CompilerParams(
            dimension_semantics=("parallel", "arbitrary"),
            vmem_limit_bytes=VMEM_LIMIT),
        name=name,
    )(*a_list, *([w] * n_a), res)


def _segmented_cumsum(x, row, seg):
    pos = row % seg
    shift = 1
    while shift < seg:
        x = x + jnp.where(pos >= shift, pltpu.roll(x, shift, axis=0), 0.0)
        shift *= 2
    return x


def _hgrn_block(q_ref, f_ref, i_ref, g_ref, lb_ref, nw_ref, o_ref,
                state_ref, qf_ref, qt_ref, k_ref, b_ref, oacc_ref, *, rows):
    n_chunks = rows // CHUNK
    n_sub = CHUNK // SUB

    lbr = lb_ref[...]
    lbe = jnp.exp(lbr - jnp.max(lbr, axis=0, keepdims=True))
    lb = lbe[0:1, :] / jnp.sum(lbe, axis=0, keepdims=True)

    qf = _silu(q_ref[...].astype(jnp.float32))
    f = lb + (1.0 - lb) * (1.0 / (1.0 + jnp.exp(-f_ref[...].astype(jnp.float32))))
    row = lax.broadcasted_iota(jnp.int32, (rows, HG_D), 0)
    b = _segmented_cumsum(jnp.log2(f), row, SUB)
    qf_ref[...] = qf
    qt_ref[...] = qf * jnp.exp2(b)
    k_ref[...] = 1.0 - f
    b_ref[...] = b

    r_i = lax.broadcasted_iota(jnp.int32, (CHUNK, CHUNK), 0)
    c_i = lax.broadcasted_iota(jnp.int32, (CHUNK, CHUNK), 1)
    causal = c_i <= r_i
    off_diag = c_i < (r_i // SUB) * SUB
    lane8 = lax.broadcasted_iota(jnp.int32, (8, CHUNK), 1)
    lane_bits = [(lane8 & (1 << n)) != 0 for n in range(3)]
    mid_block = ((r_i // 8) % 2 == 1) & (c_i // 8 == r_i // 8 - 1)

    def chunk_matmuls(c):
        r0 = c * CHUNK
        sl = pl.ds(r0, CHUNK)
        vc = i_ref[sl, :]
        qt_c = qt_ref[sl, :]

        q_blk, k_blk, b_blk, tot, khat = [], [], [], [], []
        for blk in range(n_sub):
            bs = pl.ds(r0 + blk * SUB, SUB)
            q_blk.append(qf_ref[bs, :])
            k_blk.append(k_ref[bs, :])
            b_blk.append(b_ref[bs, :])
            tot.append(b_ref[pl.ds(r0 + blk * SUB + SUB - 1, 1), :])
            khat.append(k_blk[blk] * jnp.exp2(tot[blk] - b_blk[blk]))

        def span(lo, hi):
            return functools.reduce(lambda a, t: a + t, tot[lo + 1:hi], tot[lo])

        def keys_seen_from(blk):
            parts = [khat[j] * jnp.exp2(span(j + 1, blk)) if j < blk - 1 else khat[j]
                     for j in range(n_sub)]
            return jnp.concatenate(parts, axis=0)

        k_stack = jnp.concatenate([keys_seen_from(blk) for blk in range(1, n_sub)],
                                  axis=0).astype(jnp.bfloat16)
        s_all = lax.dot_general(qt_c.astype(jnp.bfloat16), k_stack, _NT,
                                preferred_element_type=jnp.float32)

        q_mid, k_mid = [], []
        for blk in range(n_sub):
            mid = b_ref[pl.ds(r0 + blk * SUB + 7, 1), :]
            q_mid += [q_blk[blk][0:8, :], q_blk[blk][8:16, :] * jnp.exp2(b_blk[blk][8:16, :] - mid)]
            k_mid += [k_blk[blk][0:8, :] * jnp.exp2(mid - b_blk[blk][0:8, :]), k_blk[blk][8:16, :]]
        s_mid = lax.dot_general(jnp.concatenate(q_mid, axis=0).astype(jnp.bfloat16),
                                jnp.concatenate(k_mid, axis=0).astype(jnp.bfloat16), _NT,
                                preferred_element_type=jnp.float32)

        upd = jnp.dot(vc.astype(jnp.float32).T.astype(jnp.bfloat16),
                      keys_seen_from(n_sub).astype(jnp.bfloat16),
                      preferred_element_type=jnp.float32)

        q_state = jnp.concatenate(
            [qt_c[0:SUB, :]] + [qt_c[blk * SUB:(blk + 1) * SUB, :] * jnp.exp2(span(0, blk))
                                for blk in range(1, n_sub)], axis=0).astype(jnp.bfloat16)
        return dict(r0=r0, sl=sl, vc=vc, q_blk=q_blk, b_blk=b_blk,
                    s_all=s_all, s_mid=s_mid, upd=upd, q_state=q_state,
                    decay=jnp.exp2(span(0, n_sub)))

    def chunk_scores(w):
        r0 = w["r0"]
        off_rows = [jnp.zeros((SUB, CHUNK), jnp.float32)]
        for blk in range(1, n_sub):
            off_rows.append(w["s_all"][blk * SUB:(blk + 1) * SUB, (blk - 1) * CHUNK:blk * CHUNK])
        s_off = jnp.concatenate(off_rows, axis=0)

        diag_rows = []
        for blk in range(n_sub):
            for half in range(SUB // 8):
                q_h = w["q_blk"][blk][half * 8:(half + 1) * 8, :]
                b_h = w["b_blk"][blk][half * 8:(half + 1) * 8, :]
                cols = []
                for s in range(half * 8, (half + 1) * 8):
                    row_s = pl.ds(r0 + blk * SUB + s, 1)
                    d = b_h - b_ref[row_s, :]
                    if s > half * 8:
                        d = jnp.minimum(d, 0.0)
                    cols.append(jnp.sum(q_h * k_ref[row_s, :] * jnp.exp2(d),
                                        axis=-1, keepdims=True))
                bit = 0
                while len(cols) > 1:
                    cols = [jnp.where(lane_bits[bit], cols[n + 1], cols[n])
                            for n in range(0, len(cols), 2)]
                    bit += 1
                diag_rows.append(cols[0])
        s_diag = jnp.concatenate(diag_rows, axis=0)

        return jnp.where(
            causal, jnp.where(off_diag, s_off, jnp.where(mid_block, w["s_mid"], s_diag)),
            0.0).astype(jnp.bfloat16)

    def group_body(g, carry):
        work = [chunk_matmuls(g * GROUP + n) for n in range(GROUP)]
        scores = [chunk_scores(w) for w in work]
        state_t = state_ref[...]
        for w, sc in zip(work, scores):
            o_c = jnp.dot(sc, w["vc"], preferred_element_type=jnp.float32)
            o_c = o_c + lax.dot_general(w["q_state"], state_t.astype(jnp.bfloat16), _NT,
                                        preferred_element_type=jnp.float32)
            oacc_ref[w["sl"], :] = o_c
            state_t = state_t * w["decay"] + w["upd"]
        state_ref[...] = state_t
        return carry

    for g in range(n_chunks // GROUP):
        group_body(g, 0)

    o = oacc_ref[...]
    o = o * _rms_scale(o) * nw_ref[...]
    o_ref[...] = (o * _silu(g_ref[...].astype(jnp.float32))).astype(o_ref.dtype)


def _rope(u, cos_t, sin_t):
    return u * cos_t + pltpu.roll(u, LANE // 2, axis=1) * sin_t


def _rope_table_kernel(pos_ref, invf_ref, cos_ref, sin_ref):
    ang = pos_ref[...].astype(jnp.float32) * invf_ref[...]
    cos_ref[...] = jnp.cos(ang)
    sin_ref[...] = jnp.sin(ang)


def _rope_table(pos_row, invf_col, *, tn):
    m = pos_row.shape[1]
    nf = invf_col.shape[0]
    return pl.pallas_call(
        _rope_table_kernel,
        grid=(m // tn,),
        in_specs=[pl.BlockSpec((1, tn), lambda i: (0, i)),
                  pl.BlockSpec((nf, 1), lambda i: (0, 0))],
        out_specs=[pl.BlockSpec((nf, tn), lambda i: (0, i)),
                   pl.BlockSpec((nf, tn), lambda i: (0, i))],
        out_shape=[jax.ShapeDtypeStruct((nf, m), jnp.float32),
                   jax.ShapeDtypeStruct((nf, m), jnp.float32)],
        compiler_params=pltpu.CompilerParams(dimension_semantics=("parallel",)),
        name="rope_table",
    )(pos_row, invf_col)


def _mla_prep_kernel(cq_ref, ckv_ref, kr_ref, cos_ref, sin_ref, qnw_ref, kvnw_ref, wuq_ref,
                     wukv_ref, qhw_ref, khw_ref, q_ref, k_ref, v_ref):
    cq = cq_ref[...].astype(jnp.float32)
    ckv = ckv_ref[...].astype(jnp.float32)
    cqn = (cq * _rms_scale(cq) * qnw_ref[...]).astype(jnp.bfloat16)
    ckvn = (ckv * _rms_scale(ckv) * kvnw_ref[...]).astype(jnp.bfloat16)

    half = QK_ROPE // 2
    lane = lax.broadcasted_iota(jnp.int32, (1, LANE), 1)
    hi = (lane >= LANE // 2) & (lane < LANE // 2 + half)
    kr_raw = kr_ref[...].astype(jnp.float32)
    kr = jnp.where(lane < half, kr_raw, 0.0) + jnp.where(
        hi, pltpu.roll(kr_raw, LANE // 2 - half, axis=1), 0.0)

    rows = cos_ref.shape[1]
    zpad = jnp.zeros((LANE - half, rows), jnp.float32)
    cos_lo = jnp.concatenate([cos_ref[...], zpad], axis=0).T
    sin_lo = jnp.concatenate([sin_ref[...], zpad], axis=0).T
    cos_t = cos_lo + pltpu.roll(cos_lo, LANE // 2, axis=1)
    sin_t = pltpu.roll(sin_lo, LANE // 2, axis=1) - sin_lo

    qhw = qhw_ref[...]
    khw = khw_ref[...]
    kr_ss = jnp.sum(kr * kr, axis=-1, keepdims=True)
    kr_rot = _rope(kr * khw[:, LANE:QK_PAD], cos_t, sin_t)
    for h in range(MLA_HEADS):
        c0 = h * QK_PAD
        q = jnp.dot(cqn, wuq_ref[:, c0:c0 + QK_PAD], preferred_element_type=jnp.float32)
        qn = q[:, 0:LANE]
        qr = q[:, LANE:QK_PAD]
        ss = jnp.sum(qn * qn + qr * qr, axis=-1, keepdims=True)
        inv = lax.rsqrt(ss * (1.0 / QK_DIM) + NORM_EPS) * (ATTN_SCALE * LOG2_E)
        q_ref[:, c0:c0 + LANE] = (qn * inv * qhw[:, 0:LANE]).astype(q_ref.dtype)
        q_ref[:, c0 + LANE:c0 + QK_PAD] = _rope(qr * inv * qhw[:, LANE:QK_PAD],
                                                cos_t, sin_t).astype(q_ref.dtype)

        kv = jnp.dot(ckvn, wukv_ref[:, c0:c0 + QK_PAD], preferred_element_type=jnp.float32)
        kn = kv[:, 0:LANE]
        ss = jnp.sum(kn * kn, axis=-1, keepdims=True) + kr_ss
        inv = lax.rsqrt(ss * (1.0 / QK_DIM) + NORM_EPS)
        k_ref[:, c0:c0 + LANE] = (kn * inv * khw[:, 0:LANE]).astype(k_ref.dtype)
        k_ref[:, c0 + LANE:c0 + QK_PAD] = (kr_rot * inv).astype(k_ref.dtype)
        v_ref[:, c0:c0 + V_DIM] = kv[:, LANE:QK_PAD].astype(v_ref.dtype)
        v_ref[:, c0 + V_DIM:c0 + 2 * V_DIM] = jnp.ones((v_ref.shape[0], V_DIM), v_ref.dtype)


def _mla_prep(proj, cos_tab, sin_tab, qnw, kvnw, wuq, wukv, qhw, khw, *, tm):
    m = proj.shape[0]
    full = lambda a: pl.BlockSpec(a.shape, lambda i: (0, 0))
    hg_cols = 4 * HG_WIDTH
    nf = cos_tab.shape[0]
    return pl.pallas_call(
        _mla_prep_kernel,
        grid=(m // tm,),
        in_specs=[pl.BlockSpec((tm, Q_LORA), lambda i: (i, hg_cols // Q_LORA)),
                  pl.BlockSpec((tm, KV_LORA), lambda i: (i, (hg_cols + Q_LORA) // KV_LORA)),
                  pl.BlockSpec((tm, LANE), lambda i: (i, (hg_cols + Q_LORA + KV_LORA) // LANE)),
                  pl.BlockSpec((nf, tm), lambda i: (0, i)),
                  pl.BlockSpec((nf, tm), lambda i: (0, i)),
                  full(qnw), full(kvnw), full(wuq), full(wukv), full(qhw), full(khw)],
        out_specs=[pl.BlockSpec((tm, MLA_HEADS * QK_PAD), lambda i: (i, 0)),
                   pl.BlockSpec((tm, MLA_HEADS * QK_PAD), lambda i: (i, 0)),
                   pl.BlockSpec((tm, MLA_HEADS * 2 * V_DIM), lambda i: (i, 0))],
        out_shape=[jax.ShapeDtypeStruct((m, MLA_HEADS * QK_PAD), jnp.bfloat16),
                   jax.ShapeDtypeStruct((m, MLA_HEADS * QK_PAD), jnp.bfloat16),
                   jax.ShapeDtypeStruct((m, MLA_HEADS * 2 * V_DIM), jnp.bfloat16)],
        compiler_params=pltpu.CompilerParams(
            dimension_semantics=("parallel",),
            vmem_limit_bytes=VMEM_LIMIT),
        name="mla_prep",
    )(proj, proj, proj, cos_tab, sin_tab, qnw, kvnw, wuq, wukv, qhw, khw)


def _mixer_kernel(q_ref, k_ref, v_ref, hq_ref, hf_ref, hi_ref, hg_ref, lb_ref, nw_ref,
                  o_ref, oa_ref, sa_ref, sb_ref, m_ref, acc_ref,
                  state_ref, qf_ref, qt_ref, kk_ref, b_ref, oacc_ref, *, tile, n_tiles, hg_rows):
    n_col = tile // LANE
    half = tile // 2
    t = pl.program_id(2)

    def q_rows(i, start=0, size=tile):
        return pl.ds(pl.multiple_of(i * tile, tile) + start, size)

    def scores(i, j, s_ref):
        ks = pl.ds(pl.multiple_of(j * tile, tile), tile)
        s_ref[...] = lax.dot_general(q_ref[q_rows(i), :], k_ref[ks, :], _NT,
                                     preferred_element_type=jnp.float32)

    def start_tile():
        m_ref[...] = jnp.full_like(m_ref, -jnp.inf)
        acc_ref[...] = jnp.zeros_like(acc_ref)

    def finish_tile(i):
        acc = acc_ref[...]
        o_ref[q_rows(i), :] = (acc[:, :V_DIM] / acc[:, V_DIM:]).astype(o_ref.dtype)

    def diag_scores(i, s_ref):
        for rows, n_keys in ((slice(0, half), half), (slice(half, tile), tile)):
            ks = pl.ds(pl.multiple_of(i * tile, tile), n_keys)
            s_ref[rows, 0:n_keys] = lax.dot_general(
                q_ref[q_rows(i, rows.start, half), :], k_ref[ks, :], _NT,
                preferred_element_type=jnp.float32)

    def hgrn_rows(n):
        view = lambda ref: ref.at[pl.ds(pl.multiple_of(n * hg_rows, hg_rows), hg_rows), :]
        _hgrn_block(view(hq_ref), view(hf_ref), view(hi_ref), view(hg_ref), lb_ref, nw_ref,
                    view(oa_ref), state_ref, qf_ref, qt_ref, kk_ref, b_ref, oacc_ref, rows=hg_rows)

    def softmax_rows(s_ref, rows, n_vis, half_last, n_out):
        cols = [s_ref[rows, c * LANE:(c + 1) * LANE] for c in range(n_vis)]
        if half_last:
            first_chunk = lax.broadcasted_iota(jnp.int32, (1, LANE), 1) < CHUNK
            cols[-1] = jnp.where(first_chunk, cols[-1], -jnp.inf)
        m_prev = m_ref[rows, :]
        m_cur = functools.reduce(jnp.maximum, cols)
        m_new = jnp.maximum(m_prev, jnp.max(m_cur, axis=-1, keepdims=True))
        alpha = jnp.exp2(m_prev - m_new)
        p = [jnp.exp2(s_c - m_new) for s_c in cols]
        p += [jnp.zeros_like(p[0])] * (n_out - n_vis)
        m_ref[rows, :] = m_new
        return jnp.concatenate(p, axis=1).astype(v_ref.dtype), alpha

    def accumulate(rows, alpha, pv):
        acc_ref[rows, :] = jnp.concatenate([alpha, alpha], axis=1) * acc_ref[rows, :] + pv

    def softmax_pv(j, s_ref):
        ks = pl.ds(pl.multiple_of(j * tile, tile), tile)
        p, alpha = softmax_rows(s_ref, slice(None), n_col, False, n_col)
        accumulate(slice(None), alpha,
                   jnp.dot(p, v_ref[ks, :], preferred_element_type=jnp.float32))

    def diag_softmax_pv(i, s_ref):
        lane_chunks = LANE // CHUNK
        for rows, n_keys in ((slice(0, half), half), (slice(half, tile), tile)):
            parts = [softmax_rows(s_ref, slice(c * CHUNK, (c + 1) * CHUNK),
                                  c // lane_chunks + 1, c % lane_chunks != lane_chunks - 1,
                                  n_keys // LANE)
                     for c in range(rows.start // CHUNK, rows.stop // CHUNK)]
            p = jnp.concatenate([pp for pp, _ in parts], axis=0)
            alpha = jnp.concatenate([aa for _, aa in parts], axis=0)
            ks = pl.ds(pl.multiple_of(i * tile, tile), n_keys)
            accumulate(rows, alpha, jnp.dot(p, v_ref[ks, :], preferred_element_type=jnp.float32))

    def query_tile(i, odd, first, other, next_i, hg_first, hg_tail):
        start_tile()

        def pair(jj, carry):
            j = 2 * jj
            scores(i, j + 1, other)
            softmax_pv(j, first)
            scores(i, j + 2, first)
            softmax_pv(j + 1, other)
            hgrn_rows(hg_first + jj)
            return carry

        lax.fori_loop(0, i // 2, pair, 0)
        last, spare = (other, first) if odd else (first, other)
        if odd:
            diag_scores(i, other)
            softmax_pv(i - 1, first)
        if next_i is not None:
            scores(next_i, 0, spare)
        if hg_tail is not None:
            hgrn_rows(hg_tail)
        diag_softmax_pv(i, last)
        finish_tile(i)

    @pl.when(t == 0)
    def _():
        state_ref[...] = jnp.zeros_like(state_ref)

    lo = t
    hi = n_tiles - 1 - t
    hg_last = 2 * tile // hg_rows - 1

    def both_tiles(lo_odd):
        scores(lo, 0, sa_ref)
        query_tile(lo, lo_odd, sa_ref, sb_ref, hi, 0, None)
        hi_first, hi_other = (sa_ref, sb_ref) if lo_odd else (sb_ref, sa_ref)
        query_tile(hi, not lo_odd, hi_first, hi_other, None, lo // 2, hg_last)

    @pl.when(t % 2 == 0)
    def _():
        both_tiles(False)

    @pl.when(t % 2 == 1)
    def _():
        both_tiles(True)


def _mixers(q, k, v1, proj, hgrn_lb, norm_w, *, batch, seq, tile, hg_rows):
    n_tiles = seq // tile
    steps = n_tiles // 2
    rows = seq // steps
    assert n_tiles % 4 == 0 and rows // hg_rows == (n_tiles - 1) // 2 + 1
    vw = 2 * V_DIM
    hb = HG_HEADS

    def hgrn_spec(group):
        return pl.BlockSpec((rows, HG_D), lambda b, h, g: (b * steps + g, group * hb + h))

    scratch = [pltpu.VMEM((tile, tile), jnp.float32),
               pltpu.VMEM((tile, tile), jnp.float32),
               pltpu.VMEM((tile, LANE), jnp.float32),
               pltpu.VMEM((tile, vw), jnp.float32),
               pltpu.VMEM((HG_D, HG_D), jnp.float32)]
    scratch += [pltpu.VMEM((hg_rows, HG_D), jnp.float32) for _ in range(5)]
    out_sds = jax.ShapeDtypeStruct((batch * seq, MLA_HEADS * V_DIM), jnp.bfloat16)
    return pl.pallas_call(
        functools.partial(_mixer_kernel, tile=tile, n_tiles=n_tiles, hg_rows=hg_rows),
        grid=(batch, MLA_HEADS, steps),
        in_specs=[pl.BlockSpec((seq, QK_PAD), lambda b, h, g: (b, h)),
                  pl.BlockSpec((seq, QK_PAD), lambda b, h, g: (b, h)),
                  pl.BlockSpec((seq, vw), lambda b, h, g: (b, h)),
                  hgrn_spec(0), hgrn_spec(1), hgrn_spec(2), hgrn_spec(3),
                  pl.BlockSpec((hgrn_lb.shape[0], HG_D), lambda b, h, g: (0, h)),
                  pl.BlockSpec((1, HG_D), lambda b, h, g: (0, 0))],
        out_specs=[pl.BlockSpec((seq, V_DIM), lambda b, h, g: (b, h)),
                   pl.BlockSpec((rows, V_DIM), lambda b, h, g: (b * steps + g, h))],
        out_shape=[out_sds, out_sds],
        scratch_shapes=scratch,
        compiler_params=pltpu.CompilerParams(
            dimension_semantics=("parallel", "parallel", "arbitrary"),
            vmem_limit_bytes=VMEM_LIMIT),
        name="token_mixers",
    )(q, k, v1, proj, proj, proj, proj, hgrn_lb, norm_w)


def _rope_lanes(a):
    half = QK_ROPE // 2
    z = jnp.zeros(a.shape[:-1] + (LANE // 2 - half,), a.dtype)
    return jnp.concatenate([a[..., :half], z, a[..., half:], z], axis=-1)


def _head_lanes(a):
    return jnp.concatenate([a[..., :QK_NOPE], _rope_lanes(a[..., QK_NOPE:])], axis=-1)


def kernel(x, positions, attn_norm_w, w_in, hgrn_lb, hgrn_norm_w, mla_q_norm_w, w_uq,
           mla_kv_norm_w, w_ukv, q_head_norm_w, k_head_norm_w, w_out, ffn_norm_w,
           w_gate_up, w_down):
    batch, seq, d = x.shape
    m = batch * seq
    bf = jnp.bfloat16
    layer = 0
    xf = x.reshape(m, d)

    hg_cols = 4 * HG_WIDTH
    w_in_l = w_in[layer]
    w_proj = jnp.pad(w_in_l, ((0, 0), (0, PROJ_COLS - w_in_l.shape[1]))).astype(bf)
    wuq = _head_lanes(w_uq[layer].reshape(Q_LORA, MLA_HEADS, QK_DIM)).reshape(
        Q_LORA, MLA_HEADS * QK_PAD).astype(bf)
    wukv = w_ukv[layer].astype(bf)
    qhw = _head_lanes(q_head_norm_w[layer])[None, :]
    khw = _head_lanes(k_head_norm_w[layer])[None, :]
    inv_freq = 1.0 / (ROPE_THETA ** (jnp.arange(0, QK_ROPE, 2, dtype=jnp.float32) / QK_ROPE))
    nw_attn = attn_norm_w[layer][None, :]

    proj = _norm_matmul(xf, nw_attn, w_proj, tm=1024, tn=PROJ_COLS // 7,
                        out_dtype=bf, name="in_proj")

    cos_tab, sin_tab = _rope_table(positions.reshape(1, m), inv_freq[:, None], tn=2048)
    q, k, v = _mla_prep(proj, cos_tab, sin_tab,
                        mla_q_norm_w[layer][None, :], mla_kv_norm_w[layer][None, :],
                        wuq, wukv, qhw, khw, tm=512)
    o_b, o_a = _mixers(q, k, v, proj, hgrn_lb, hgrn_norm_w[layer][None, :],
                       batch=batch, seq=seq, tile=1024, hg_rows=GROUP * CHUNK)

    x1 = _res_matmul([o_a, o_b], w_out[layer].astype(bf), xf, tm=1024, tn=512, name="out_proj")

    act = _norm_swiglu(x1, ffn_norm_w[layer][None, :], w_gate_up[layer], tm=1024, tn=512)
    x2 = _res_matmul([act], w_down[layer].astype(bf), x1, tm=1024, tn=512, name="ffn_down")
    return x2.reshape(batch, seq, d)
```

```python
import functools

import jax
import jax.numpy as jnp
import numpy as np
from jax import lax
from jax.experimental import pallas as pl
from jax.experimental.pallas import tpu as pltpu

D_MODEL = 2048
CHUNK = 64
SUB = 16
GROUP = 8
HG_WIDTH = 1024
HG_HEADS = 8
HG_D = 128
MLA_HEADS = 8
V_DIM = 128
QK_NOPE = 128
QK_ROPE = 64
QK_DIM = QK_NOPE + QK_ROPE
QK_PAD = 256
Q_LORA = 512
KV_LORA = 512
ROPE_THETA = 10000.0
ATTN_SCALE = QK_DIM ** -0.5
LOG2_E = 1.4426950408889634
D_FF = 5632
NORM_EPS = 1e-6
LANE = 128

VMEM_LIMIT = 56 * 1024 * 1024

_NT = (((1,), (1,)), ((), ()))


def _silu(x):
    return x * (1.0 / (1.0 + jnp.exp(-x)))


def _rms_scale(x):
    return lax.rsqrt(jnp.mean(x * x, axis=-1, keepdims=True) + NORM_EPS)


def _norm_matmul_kernel(x_ref, nw_ref, w_ref, wt_ref, o_ref, ot_ref, h_ref):
    @pl.when(pl.program_id(1) == 0)
    def _():
        x = x_ref[...]
        h = (x * _rms_scale(x) * nw_ref[...]).astype(h_ref.dtype)
        h_ref[...] = h
        ot_ref[...] = jnp.dot(h, wt_ref[...], preferred_element_type=jnp.float32).astype(ot_ref.dtype)

    o_ref[...] = jnp.dot(h_ref[...], w_ref[...].astype(h_ref.dtype),
                         preferred_element_type=jnp.float32).astype(o_ref.dtype)


def _norm_matmul(x, nw, w, w_tail, *, n_main, tm, tn, out_dtype, name):
    m, k = x.shape
    nt = w_tail.shape[1]
    return pl.pallas_call(
        _norm_matmul_kernel,
        grid=(m // tm, n_main // tn),
        in_specs=[pl.BlockSpec((tm, k), lambda i, j: (i, 0)),
                  pl.BlockSpec((1, k), lambda i, j: (0, 0)),
                  pl.BlockSpec((k, tn), lambda i, j: (0, j)),
                  pl.BlockSpec((k, nt), lambda i, j: (0, 0))],
        out_specs=[pl.BlockSpec((tm, tn), lambda i, j: (i, j)),
                   pl.BlockSpec((tm, nt), lambda i, j: (i, 0))],
        out_shape=[jax.ShapeDtypeStruct((m, n_main), out_dtype),
                   jax.ShapeDtypeStruct((m, nt), out_dtype)],
        scratch_shapes=[pltpu.VMEM((tm, k), jnp.bfloat16)],
        compiler_params=pltpu.CompilerParams(
            dimension_semantics=("parallel", "arbitrary"),
            vmem_limit_bytes=VMEM_LIMIT),
        name=name,
    )(x, nw, w, w_tail)


def _norm_swiglu_kernel(x_ref, nw_ref, wg_ref, wu_ref, o_ref, h_ref):
    @pl.when(pl.program_id(1) == 0)
    def _():
        x = x_ref[...]
        h_ref[...] = (x * _rms_scale(x) * nw_ref[...]).astype(h_ref.dtype)

    h = h_ref[...]
    g = jnp.dot(h, wg_ref[...].astype(h.dtype), preferred_element_type=jnp.float32)
    u = jnp.dot(h, wu_ref[...].astype(h.dtype), preferred_element_type=jnp.float32)
    o_ref[...] = (_silu(g) * u).astype(o_ref.dtype)


def _norm_swiglu(x, nw, w_gate_up, *, tm, tn):
    m, k = x.shape
    n = w_gate_up.shape[1] // 2
    up_off = n // tn
    return pl.pallas_call(
        _norm_swiglu_kernel,
        grid=(m // tm, n // tn),
        in_specs=[pl.BlockSpec((tm, k), lambda i, j: (i, 0)),
                  pl.BlockSpec((1, k), lambda i, j: (0, 0)),
                  pl.BlockSpec((k, tn), lambda i, j: (0, j)),
                  pl.BlockSpec((k, tn), lambda i, j: (0, j + up_off))],
        out_specs=pl.BlockSpec((tm, tn), lambda i, j: (i, j)),
        out_shape=jax.ShapeDtypeStruct((m, n), jnp.bfloat16),
        scratch_shapes=[pltpu.VMEM((tm, k), jnp.bfloat16)],
        compiler_params=pltpu.CompilerParams(
            dimension_semantics=("parallel", "arbitrary"),
            vmem_limit_bytes=VMEM_LIMIT),
        name="ffn_up",
    )(x, nw, w_gate_up, w_gate_up)


def _res_matmul_kernel(*refs, n_a):
    a_refs = refs[:n_a]
    w_refs = refs[n_a:2 * n_a]
    r_ref = refs[2 * n_a]
    o_ref = refs[2 * n_a + 1]
    acc = r_ref[...]
    for a_ref, w_ref in zip(a_refs, w_refs):
        acc = acc + jnp.dot(a_ref[...], w_ref[...], preferred_element_type=jnp.float32)
    o_ref[...] = acc


def _res_matmul(a_list, w, res, *, tm, tn, name):
    m, n = res.shape
    n_a = len(a_list)
    ka = a_list[0].shape[1]
    in_specs = [pl.BlockSpec((tm, ka), lambda i, j: (i, 0)) for _ in a_list]
    in_specs += [pl.BlockSpec((ka, tn), functools.partial(lambda i, j, kb: (kb, j), kb=kb))
                 for kb in range(n_a)]
    in_specs += [pl.BlockSpec((tm, tn), lambda i, j: (i, j))]
    return pl.pallas_call(
        functools.partial(_res_matmul_kernel, n_a=n_a),
        grid=(m // tm, n // tn),
        in_specs=in_specs,
        out_specs=pl.BlockSpec((tm, tn), lambda i, j: (i, j)),
        out_shape=jax.ShapeDtypeStruct((m, n), jnp.float32),
        compiler_params=pltpu.CompilerParams(
            dimension_semantics=("parallel", "arbitrary"),
            vmem_limit_bytes=VMEM_LIMIT),
        name=name,
    )(*a_list, *([w] * n_a), res)


def _segmented_cumsum(x, row, seg):
    pos = row % seg
    shift = 1
    while shift < seg:
        x = x + jnp.where(pos >= shift, pltpu.roll(x, shift, axis=0), 0.0)
        shift *= 2
    return x


def _hgrn_block(q_ref, f_ref, i_ref, g_ref, lb_ref, nw_ref, o_ref,
                state_ref, qf_ref, qt_ref, k_ref, b_ref, oacc_ref, *, rows):
    n_chunks = rows // CHUNK
    n_sub = CHUNK // SUB

    lbr = lb_ref[...]
    lbe = jnp.exp(lbr - jnp.max(lbr, axis=0, keepdims=True))
    lb = lbe[0:1, :] / jnp.sum(lbe, axis=0, keepdims=True)

    qf = _silu(q_ref[...].astype(jnp.float32))
    f = lb + (1.0 - lb) * (1.0 / (1.0 + jnp.exp(-f_ref[...].astype(jnp.float32))))
    row = lax.broadcasted_iota(jnp.int32, (rows, HG_D), 0)
    b = _segmented_cumsum(jnp.log2(f), row, SUB)
    qf_ref[...] = qf
    qt_ref[...] = qf * jnp.exp2(b)
    k_ref[...] = 1.0 - f
    b_ref[...] = b

    r_i = lax.broadcasted_iota(jnp.int32, (CHUNK, CHUNK), 0)
    c_i = lax.broadcasted_iota(jnp.int32, (CHUNK, CHUNK), 1)
    causal = c_i <= r_i
    off_diag = c_i < (r_i // SUB) * SUB
    lane8 = lax.broadcasted_iota(jnp.int32, (8, CHUNK), 1)
    lane_bits = [(lane8 & (1 << n)) != 0 for n in range(3)]
    mid_block = ((r_i // 8) % 2 == 1) & (c_i // 8 == r_i // 8 - 1)

    def chunk_matmuls(c):
        r0 = c * CHUNK
        sl = pl.ds(r0, CHUNK)
        vc = i_ref[sl, :]
        qt_c = qt_ref[sl, :]

        q_blk, k_blk, b_blk, tot, khat = [], [], [], [], []
        for blk in range(n_sub):
            bs = pl.ds(r0 + blk * SUB, SUB)
            q_blk.append(qf_ref[bs, :])
            k_blk.append(k_ref[bs, :])
            b_blk.append(b_ref[bs, :])
            tot.append(b_ref[pl.ds(r0 + blk * SUB + SUB - 1, 1), :])
            khat.append(k_blk[blk] * jnp.exp2(tot[blk] - b_blk[blk]))

        def span(lo, hi):
            return functools.reduce(lambda a, t: a + t, tot[lo + 1:hi], tot[lo])

        def keys_seen_from(blk):
            parts = [khat[j] * jnp.exp2(span(j + 1, blk)) if j < blk - 1 else khat[j]
                     for j in range(n_sub)]
            return jnp.concatenate(parts, axis=0)

        k_stack = jnp.concatenate([keys_seen_from(blk) for blk in range(1, n_sub)],
                                  axis=0).astype(jnp.bfloat16)
        s_all = lax.dot_general(qt_c.astype(jnp.bfloat16), k_stack, _NT,
                                preferred_element_type=jnp.float32)

        q_mid, k_mid = [], []
        for blk in range(n_sub):
            mid = b_ref[pl.ds(r0 + blk * SUB + 7, 1), :]
            q_mid += [q_blk[blk][0:8, :], q_blk[blk][8:16, :] * jnp.exp2(b_blk[blk][8:16, :] - mid)]
            k_mid += [k_blk[blk][0:8, :] * jnp.exp2(mid - b_blk[blk][0:8, :]), k_blk[blk][8:16, :]]
        s_mid = lax.dot_general(jnp.concatenate(q_mid, axis=0).astype(jnp.bfloat16),
                                jnp.concatenate(k_mid, axis=0).astype(jnp.bfloat16), _NT,
                                preferred_element_type=jnp.float32)

        upd = jnp.dot(vc.astype(jnp.float32).T.astype(jnp.bfloat16),
                      keys_seen_from(n_sub).astype(jnp.bfloat16),
                      preferred_element_type=jnp.float32)

        q_state = jnp.concatenate(
            [qt_c[0:SUB, :]] + [qt_c[blk * SUB:(blk + 1) * SUB, :] * jnp.exp2(span(0, blk))
                                for blk in range(1, n_sub)], axis=0).astype(jnp.bfloat16)
        return dict(r0=r0, sl=sl, vc=vc, q_blk=q_blk, b_blk=b_blk,
                    s_all=s_all, s_mid=s_mid, upd=upd, q_state=q_state,
                    decay=jnp.exp2(span(0, n_sub)))

    def chunk_scores(w):
        r0 = w["r0"]
        off_rows = [jnp.zeros((SUB, CHUNK), jnp.float32)]
        for blk in range(1, n_sub):
            off_rows.append(w["s_all"][blk * SUB:(blk + 1) * SUB, (blk - 1) * CHUNK:blk * CHUNK])
        s_off = jnp.concatenate(off_rows, axis=0)

        diag_rows = []
        for blk in range(n_sub):
            for half in range(SUB // 8):
                q_h = w["q_blk"][blk][half * 8:(half + 1) * 8, :]
                b_h = w["b_blk"][blk][half * 8:(half + 1) * 8, :]
                cols = []
                for s in range(half * 8, (half + 1) * 8):
                    row_s = pl.ds(r0 + blk * SUB + s, 1)
                    d = b_h - b_ref[row_s, :]
                    if s > half * 8:
                        d = jnp.minimum(d, 0.0)
                    cols.append(jnp.sum(q_h * k_ref[row_s, :] * jnp.exp2(d),
                                        axis=-1, keepdims=True))
                bit = 0
                while len(cols) > 1:
                    cols = [jnp.where(lane_bits[bit], cols[n + 1], cols[n])
                            for n in range(0, len(cols), 2)]
                    bit += 1
                diag_rows.append(cols[0])
        s_diag = jnp.concatenate(diag_rows, axis=0)

        return jnp.where(
            causal, jnp.where(off_diag, s_off, jnp.where(mid_block, w["s_mid"], s_diag)),
            0.0).astype(jnp.bfloat16)

    def group_body(g, carry):
        work = [chunk_matmuls(g * GROUP + n) for n in range(GROUP)]
        scores = [chunk_scores(w) for w in work]
        state_t = state_ref[...]
        for w, sc in zip(work, scores):
            o_c = jnp.dot(sc, w["vc"], preferred_element_type=jnp.float32)
            o_c = o_c + lax.dot_general(w["q_state"], state_t.astype(jnp.bfloat16), _NT,
                                        preferred_element_type=jnp.float32)
            oacc_ref[w["sl"], :] = o_c
            state_t = state_t * w["decay"] + w["upd"]
        state_ref[...] = state_t
        return carry

    for g in range(n_chunks // GROUP):
        group_body(g, 0)

    o = oacc_ref[...]
    o = o * _rms_scale(o) * nw_ref[...]
    o_ref[...] = (o * _silu(g_ref[...].astype(jnp.float32))).astype(o_ref.dtype)


def _rope(u, cos_t, sin_t):
    return u * cos_t + pltpu.roll(u, LANE // 2, axis=1) * sin_t


def _rope_table_kernel(pos_ref, invf_ref, cos_ref, sin_ref):
    ang = pos_ref[...].astype(jnp.float32) * invf_ref[...]
    cos_ref[...] = jnp.cos(ang)
    sin_ref[...] = jnp.sin(ang)


def _rope_table(pos_row, invf_col, *, tn):
    m = pos_row.shape[1]
    nf = invf_col.shape[0]
    return pl.pallas_call(
        _rope_table_kernel,
        grid=(m // tn,),
        in_specs=[pl.BlockSpec((1, tn), lambda i: (0, i)),
                  pl.BlockSpec((nf, 1), lambda i: (0, 0))],
        out_specs=[pl.BlockSpec((nf, tn), lambda i: (0, i)),
                   pl.BlockSpec((nf, tn), lambda i: (0, i))],
        out_shape=[jax.ShapeDtypeStruct((nf, m), jnp.float32),
                   jax.ShapeDtypeStruct((nf, m), jnp.float32)],
        compiler_params=pltpu.CompilerParams(dimension_semantics=("parallel",)),
        name="rope_table",
    )(pos_row, invf_col)


def _mla_prep_kernel(cq_ref, ckv_ref, kr_ref, cos_ref, sin_ref, qnw_ref, kvnw_ref, wuq_ref,
                     wukv_ref, qhw_ref, khw_ref, q_ref, k_ref, v_ref):
    cq = cq_ref[...].astype(jnp.float32)
    ckv = ckv_ref[...].astype(jnp.float32)
    cqn = (cq * _rms_scale(cq) * qnw_ref[...]).astype(jnp.bfloat16)
    ckvn = (ckv * _rms_scale(ckv) * kvnw_ref[...]).astype(jnp.bfloat16)

    half = QK_ROPE // 2
    lane = lax.broadcasted_iota(jnp.int32, (1, LANE), 1)
    hi = (lane >= LANE // 2) & (lane < LANE // 2 + half)
    kr_raw = kr_ref[...].astype(jnp.float32)
    kr = jnp.where(lane < half, kr_raw, 0.0) + jnp.where(
        hi, pltpu.roll(kr_raw, LANE // 2 - half, axis=1), 0.0)

    rows = cos_ref.shape[1]
    zpad = jnp.zeros((LANE - half, rows), jnp.float32)
    cos_lo = jnp.concatenate([cos_ref[...], zpad], axis=0).T
    sin_lo = jnp.concatenate([sin_ref[...], zpad], axis=0).T
    cos_t = cos_lo + pltpu.roll(cos_lo, LANE // 2, axis=1)
    sin_t = pltpu.roll(sin_lo, LANE // 2, axis=1) - sin_lo

    qhw = qhw_ref[...]
    khw = khw_ref[...]
    kr_ss = jnp.sum(kr * kr, axis=-1, keepdims=True)
    kr_rot = _rope(kr * khw[:, LANE:QK_PAD], cos_t, sin_t)
    for h in range(MLA_HEADS):
        c0 = h * QK_PAD
        q = jnp.dot(cqn, wuq_ref[:, c0:c0 + QK_PAD], preferred_element_type=jnp.float32)
        qn = q[:, 0:LANE]
        qr = q[:, LANE:QK_PAD]
        ss = jnp.sum(qn * qn + qr * qr, axis=-1, keepdims=True)
        inv = lax.rsqrt(ss * (1.0 / QK_DIM) + NORM_EPS) * (ATTN_SCALE * LOG2_E)
        q_ref[:, c0:c0 + LANE] = (qn * inv * qhw[:, 0:LANE]).astype(q_ref.dtype)
        q_ref[:, c0 + LANE:c0 + QK_PAD] = _rope(qr * inv * qhw[:, LANE:QK_PAD],
                                                cos_t, sin_t).astype(q_ref.dtype)

        kv = jnp.dot(ckvn, wukv_ref[:, c0:c0 + QK_PAD], preferred_element_type=jnp.float32)
        kn = kv[:, 0:LANE]
        ss = jnp.sum(kn * kn, axis=-1, keepdims=True) + kr_ss
        inv = lax.rsqrt(ss * (1.0 / QK_DIM) + NORM_EPS)
        k_ref[:, c0:c0 + LANE] = (kn * inv * khw[:, 0:LANE]).astype(k_ref.dtype)
        k_ref[:, c0 + LANE:c0 + QK_PAD] = (kr_rot * inv).astype(k_ref.dtype)
        v_ref[:, c0:c0 + V_DIM] = kv[:, LANE:QK_PAD].astype(v_ref.dtype)
        v_ref[:, c0 + V_DIM:c0 + 2 * V_DIM] = jnp.ones((v_ref.shape[0], V_DIM), v_ref.dtype)


def _mla_prep(proj, proj_rope, cos_tab, sin_tab, qnw, kvnw, wuq, wukv, qhw, khw, *, tm):
    m = proj.shape[0]
    full = lambda a: pl.BlockSpec(a.shape, lambda i: (0, 0))
    hg_cols = 4 * HG_WIDTH
    nf = cos_tab.shape[0]
    return pl.pallas_call(
        _mla_prep_kernel,
        grid=(m // tm,),
        in_specs=[pl.BlockSpec((tm, Q_LORA), lambda i: (i, hg_cols // Q_LORA)),
                  pl.BlockSpec((tm, KV_LORA), lambda i: (i, (hg_cols + Q_LORA) // KV_LORA)),
                  pl.BlockSpec((tm, LANE), lambda i: (i, 0)),
                  pl.BlockSpec((nf, tm), lambda i: (0, i)),
                  pl.BlockSpec((nf, tm), lambda i: (0, i)),
                  full(qnw), full(kvnw), full(wuq), full(wukv), full(qhw), full(khw)],
        out_specs=[pl.BlockSpec((tm, MLA_HEADS * QK_PAD), lambda i: (i, 0)),
                   pl.BlockSpec((tm, MLA_HEADS * QK_PAD), lambda i: (i, 0)),
                   pl.BlockSpec((tm, MLA_HEADS * 2 * V_DIM), lambda i: (i, 0))],
        out_shape=[jax.ShapeDtypeStruct((m, MLA_HEADS * QK_PAD), jnp.bfloat16),
                   jax.ShapeDtypeStruct((m, MLA_HEADS * QK_PAD), jnp.bfloat16),
                   jax.ShapeDtypeStruct((m, MLA_HEADS * 2 * V_DIM), jnp.bfloat16)],
        compiler_params=pltpu.CompilerParams(
            dimension_semantics=("parallel",),
            vmem_limit_bytes=VMEM_LIMIT),
        name="mla_prep",
    )(proj, proj, proj_rope, cos_tab, sin_tab, qnw, kvnw, wuq, wukv, qhw, khw)


def _mixer_kernel(q_ref, k_ref, v_ref, hq_ref, hf_ref, hi_ref, hg_ref, lb_ref, nw_ref,
                  o_ref, oa_ref, sa_ref, sb_ref, m_ref, acc_ref,
                  state_ref, qf_ref, qt_ref, kk_ref, b_ref, oacc_ref, *, tile, n_tiles, hg_rows):
    n_col = tile // LANE
    half = tile // 2
    t = pl.program_id(2)

    def q_rows(i, start=0, size=tile):
        return pl.ds(pl.multiple_of(i * tile, tile) + start, size)

    def scores(i, j, s_ref):
        ks = pl.ds(pl.multiple_of(j * tile, tile), tile)
        s_ref[...] = lax.dot_general(q_ref[q_rows(i), :], k_ref[ks, :], _NT,
                                     preferred_element_type=jnp.float32)

    def start_tile():
        m_ref[...] = jnp.full_like(m_ref, -jnp.inf)
        acc_ref[...] = jnp.zeros_like(acc_ref)

    def finish_tile(i):
        acc = acc_ref[...]
        o_ref[q_rows(i), :] = (acc[:, :V_DIM] / acc[:, V_DIM:]).astype(o_ref.dtype)

    def diag_scores(i, s_ref):
        for rows, n_keys in ((slice(0, half), half), (slice(half, tile), tile)):
            ks = pl.ds(pl.multiple_of(i * tile, tile), n_keys)
            s_ref[rows, 0:n_keys] = lax.dot_general(
                q_ref[q_rows(i, rows.start, half), :], k_ref[ks, :], _NT,
                preferred_element_type=jnp.float32)

    def hgrn_rows(n):
        view = lambda ref: ref.at[pl.ds(pl.multiple_of(n * hg_rows, hg_rows), hg_rows), :]
        _hgrn_block(view(hq_ref), view(hf_ref), view(hi_ref), view(hg_ref), lb_ref, nw_ref,
                    view(oa_ref), state_ref, qf_ref, qt_ref, kk_ref, b_ref, oacc_ref, rows=hg_rows)

    def softmax_rows(s_ref, rows, n_vis, half_last, n_out):
        cols = [s_ref[rows, c * LANE:(c + 1) * LANE] for c in range(n_vis)]
        if half_last:
            first_chunk = lax.broadcasted_iota(jnp.int32, (1, LANE), 1) < CHUNK
            cols[-1] = jnp.where(first_chunk, cols[-1], -jnp.inf)
        m_prev = m_ref[rows, :]
        m_cur = functools.reduce(jnp.maximum, cols)
        m_new = jnp.maximum(m_prev, jnp.max(m_cur, axis=-1, keepdims=True))
        alpha = jnp.exp2(m_prev - m_new)
        p = [jnp.exp2(s_c - m_new) for s_c in cols]
        p += [jnp.zeros_like(p[0])] * (n_out - n_vis)
        m_ref[rows, :] = m_new
        return jnp.concatenate(p, axis=1).astype(v_ref.dtype), alpha

    def accumulate(rows, alpha, pv):
        acc_ref[rows, :] = jnp.concatenate([alpha, alpha], axis=1) * acc_ref[rows, :] + pv

    def softmax_pv(j, s_ref):
        ks = pl.ds(pl.multiple_of(j * tile, tile), tile)
        p, alpha = softmax_rows(s_ref, slice(None), n_col, False, n_col)
        accumulate(slice(None), alpha,
                   jnp.dot(p, v_ref[ks, :], preferred_element_type=jnp.float32))

    def diag_softmax_pv(i, s_ref):
        lane_chunks = LANE // CHUNK
        for rows, n_keys in ((slice(0, half), half), (slice(half, tile), tile)):
            parts = [softmax_rows(s_ref, slice(c * CHUNK, (c + 1) * CHUNK),
                                  c // lane_chunks + 1, c % lane_chunks != lane_chunks - 1,
                                  n_keys // LANE)
                     for c in range(rows.start // CHUNK, rows.stop // CHUNK)]
            p = jnp.concatenate([pp for pp, _ in parts], axis=0)
            alpha = jnp.concatenate([aa for _, aa in parts], axis=0)
            ks = pl.ds(pl.multiple_of(i * tile, tile), n_keys)
            accumulate(rows, alpha, jnp.dot(p, v_ref[ks, :], preferred_element_type=jnp.float32))

    def query_tile(i, odd, first, other, next_i, hg_first, hg_tail):
        start_tile()

        def pair(jj, carry):
            j = 2 * jj
            scores(i, j + 1, other)
            softmax_pv(j, first)
            scores(i, j + 2, first)
            softmax_pv(j + 1, other)
            hgrn_rows(hg_first + jj)
            return carry

        lax.fori_loop(0, i // 2, pair, 0)
        last, spare = (other, first) if odd else (first, other)
        if odd:
            diag_scores(i, other)
            softmax_pv(i - 1, first)
        if next_i is not None:
            scores(next_i, 0, spare)
        if hg_tail is not None:
            hgrn_rows(hg_tail)
        diag_softmax_pv(i, last)
        finish_tile(i)

    @pl.when(t == 0)
    def _():
        state_ref[...] = jnp.zeros_like(state_ref)

    lo = t
    hi = n_tiles - 1 - t
    hg_last = 2 * tile // hg_rows - 1

    def both_tiles(lo_odd):
        scores(lo, 0, sa_ref)
        query_tile(lo, lo_odd, sa_ref, sb_ref, hi, 0, None)
        hi_first, hi_other = (sa_ref, sb_ref) if lo_odd else (sb_ref, sa_ref)
        query_tile(hi, not lo_odd, hi_first, hi_other, None, lo // 2, hg_last)

    @pl.when(t % 2 == 0)
    def _():
        both_tiles(False)

    @pl.when(t % 2 == 1)
    def _():
        both_tiles(True)


def _mixers(q, k, v1, proj, hgrn_lb, norm_w, *, batch, seq, tile, hg_rows):
    n_tiles = seq // tile
    steps = n_tiles // 2
    rows = seq // steps
    assert n_tiles % 4 == 0 and rows // hg_rows == (n_tiles - 1) // 2 + 1
    vw = 2 * V_DIM
    hb = HG_HEADS

    def hgrn_spec(group):
        return pl.BlockSpec((rows, HG_D), lambda b, h, g: (b * steps + g, group * hb + h))

    scratch = [pltpu.VMEM((tile, tile), jnp.float32),
               pltpu.VMEM((tile, tile), jnp.float32),
               pltpu.VMEM((tile, LANE), jnp.float32),
               pltpu.VMEM((tile, vw), jnp.float32),
               pltpu.VMEM((HG_D, HG_D), jnp.float32)]
    scratch += [pltpu.VMEM((hg_rows, HG_D), jnp.float32) for _ in range(5)]
    out_sds = jax.ShapeDtypeStruct((batch * seq, MLA_HEADS * V_DIM), jnp.bfloat16)
    return pl.pallas_call(
        functools.partial(_mixer_kernel, tile=tile, n_tiles=n_tiles, hg_rows=hg_rows),
        grid=(batch, MLA_HEADS, steps),
        in_specs=[pl.BlockSpec((seq, QK_PAD), lambda b, h, g: (b, h)),
                  pl.BlockSpec((seq, QK_PAD), lambda b, h, g: (b, h)),
                  pl.BlockSpec((seq, vw), lambda b, h, g: (b, h)),
                  hgrn_spec(0), hgrn_spec(1), hgrn_spec(2), hgrn_spec(3),
                  pl.BlockSpec((hgrn_lb.shape[0], HG_D), lambda b, h, g: (0, h)),
                  pl.BlockSpec((1, HG_D), lambda b, h, g: (0, 0))],
        out_specs=[pl.BlockSpec((seq, V_DIM), lambda b, h, g: (b, h)),
                   pl.BlockSpec((rows, V_DIM), lambda b, h, g: (b * steps + g, h))],
        out_shape=[out_sds, out_sds],
        scratch_shapes=scratch,
        compiler_params=pltpu.CompilerParams(
            dimension_semantics=("parallel", "parallel", "arbitrary"),
            vmem_limit_bytes=VMEM_LIMIT),
        name="token_mixers",
    )(q, k, v1, proj, proj, proj, proj, hgrn_lb, norm_w)


def _rope_lanes(a):
    half = QK_ROPE // 2
    z = jnp.zeros(a.shape[:-1] + (LANE // 2 - half,), a.dtype)
    return jnp.concatenate([a[..., :half], z, a[..., half:], z], axis=-1)


def _head_lanes(a):
    return jnp.concatenate([a[..., :QK_NOPE], _rope_lanes(a[..., QK_NOPE:])], axis=-1)


def kernel(x, positions, attn_norm_w, w_in, hgrn_lb, hgrn_norm_w, mla_q_norm_w, w_uq,
           mla_kv_norm_w, w_ukv, q_head_norm_w, k_head_norm_w, w_out, ffn_norm_w,
           w_gate_up, w_down):
    batch, seq, d = x.shape
    m = batch * seq
    bf = jnp.bfloat16
    layer = 0
    xf = x.reshape(m, d)

    hg_cols = 4 * HG_WIDTH
    w_in_l = w_in[layer]
    mla_cols = hg_cols + Q_LORA + KV_LORA
    w_rope = jnp.pad(w_in_l[:, mla_cols:], ((0, 0), (0, LANE - QK_ROPE))).astype(bf)
    wuq = _head_lanes(w_uq[layer].reshape(Q_LORA, MLA_HEADS, QK_DIM)).reshape(
        Q_LORA, MLA_HEADS * QK_PAD).astype(bf)
    wukv = w_ukv[layer].astype(bf)
    qhw = _head_lanes(q_head_norm_w[layer])[None, :]
    khw = _head_lanes(k_head_norm_w[layer])[None, :]
    inv_freq = 1.0 / (ROPE_THETA ** (jnp.arange(0, QK_ROPE, 2, dtype=jnp.float32) / QK_ROPE))
    nw_attn = attn_norm_w[layer][None, :]

    proj, proj_rope = _norm_matmul(xf, nw_attn, w_in_l, w_rope, n_main=mla_cols, tm=1024, tn=1024,
                                   out_dtype=bf, name="in_proj")

    cos_tab, sin_tab = _rope_table(positions.reshape(1, m), inv_freq[:, None], tn=2048)
    q, k, v = _mla_prep(proj, proj_rope, cos_tab, sin_tab,
                        mla_q_norm_w[layer][None, :], mla_kv_norm_w[layer][None, :],
                        wuq, wukv, qhw, khw, tm=512)
    o_b, o_a = _mixers(q, k, v, proj, hgrn_lb, hgrn_norm_w[layer][None, :],
                       batch=batch, seq=seq, tile=1024, hg_rows=GROUP * CHUNK)

    x1 = _res_matmul([o_a, o_b], w_out[layer].astype(bf), xf, tm=512, tn=d, name="out_proj")

    act = _norm_swiglu(x1, ffn_norm_w[layer][None, :], w_gate_up[layer], tm=1024, tn=512)
    x2 = _res_matmul([act], w_down[layer].astype(bf), x1, tm=1024, tn=512, name="ffn_down")
    return x2.reshape(batch, seq, d)
```

```python
import functools

import jax
import jax.numpy as jnp
import numpy as np
from jax import lax
from jax.experimental import pallas as pl
from jax.experimental.pallas import tpu as pltpu

D_MODEL = 2048
CHUNK = 64
SUB = 16
GROUP = 8
HG_WIDTH = 1024
HG_HEADS = 8
HG_D = 128
MLA_HEADS = 8
V_DIM = 128
QK_NOPE = 128
QK_ROPE = 64
QK_DIM = QK_NOPE + QK_ROPE
QK_PAD = 256
Q_LORA = 512
KV_LORA = 512
ROPE_THETA = 10000.0
ATTN_SCALE = QK_DIM ** -0.5
LOG2_E = 1.4426950408889634
D_FF = 5632
NORM_EPS = 1e-6
LANE = 128

VMEM_LIMIT = 56 * 1024 * 1024

_NT = (((1,), (1,)), ((), ()))


def _silu(x):
    return x * (1.0 / (1.0 + jnp.exp(-x)))


def _rms_scale(x):
    return lax.rsqrt(jnp.mean(x * x, axis=-1, keepdims=True) + NORM_EPS)


def _norm_matmul_kernel(x_ref, nw_ref, w_ref, wt_ref, o_ref, ot_ref, h_ref):
    @pl.when(pl.program_id(1) == 0)
    def _():
        x = x_ref[...]
        h = (x * _rms_scale(x) * nw_ref[...]).astype(h_ref.dtype)
        h_ref[...] = h
        ot_ref[...] = jnp.dot(h, wt_ref[...], preferred_element_type=jnp.float32).astype(ot_ref.dtype)

    o_ref[...] = jnp.dot(h_ref[...], w_ref[...].astype(h_ref.dtype),
                         preferred_element_type=jnp.float32).astype(o_ref.dtype)


def _norm_matmul(x, nw, w, w_tail, *, n_main, tm, tn, out_dtype, name):
    m, k = x.shape
    nt = w_tail.shape[1]
    return pl.pallas_call(
        _norm_matmul_kernel,
        grid=(m // tm, n_main // tn),
        in_specs=[pl.BlockSpec((tm, k), lambda i, j: (i, 0)),
                  pl.BlockSpec((1, k), lambda i, j: (0, 0)),
                  pl.BlockSpec((k, tn), lambda i, j: (0, j)),
                  pl.BlockSpec((k, nt), lambda i, j: (0, 0))],
        out_specs=[pl.BlockSpec((tm, tn), lambda i, j: (i, j)),
                   pl.BlockSpec((tm, nt), lambda i, j: (i, 0))],
        out_shape=[jax.ShapeDtypeStruct((m, n_main), out_dtype),
                   jax.ShapeDtypeStruct((m, nt), out_dtype)],
        scratch_shapes=[pltpu.VMEM((tm, k), jnp.bfloat16)],
        compiler_params=pltpu.CompilerParams(
            dimension_semantics=("parallel", "arbitrary"),
            vmem_limit_bytes=VMEM_LIMIT),
        name=name,
    )(x, nw, w, w_tail)


def _swiglu_kernel(h_ref, wg_ref, wu_ref, o_ref):
    h = h_ref[...]
    g = jnp.dot(h, wg_ref[...].astype(h.dtype), preferred_element_type=jnp.float32)
    u = jnp.dot(h, wu_ref[...].astype(h.dtype), preferred_element_type=jnp.float32)
    o_ref[...] = (_silu(g) * u).astype(o_ref.dtype)


def _swiglu(h, w_gate_up, layer, *, tm, tn):
    m, k = h.shape
    n = w_gate_up.shape[2] // 2
    up_off = n // tn
    return pl.pallas_call(
        _swiglu_kernel,
        grid=(m // tm, n // tn),
        in_specs=[pl.BlockSpec((tm, k), lambda i, j: (i, 0)),
                  pl.BlockSpec((None, k, tn), lambda i, j: (layer, 0, j)),
                  pl.BlockSpec((None, k, tn), lambda i, j: (layer, 0, j + up_off))],
        out_specs=pl.BlockSpec((tm, tn), lambda i, j: (i, j)),
        out_shape=jax.ShapeDtypeStruct((m, n), jnp.bfloat16),
        compiler_params=pltpu.CompilerParams(
            dimension_semantics=("parallel", "arbitrary"),
            vmem_limit_bytes=VMEM_LIMIT),
        name="ffn_up",
    )(h, w_gate_up, w_gate_up)


def _res_matmul_kernel(*refs, n_a, with_norm):
    a_refs = refs[:n_a]
    w_refs = refs[n_a:2 * n_a]
    r_ref = refs[2 * n_a]
    acc = r_ref[...]
    for a_ref, w_ref in zip(a_refs, w_refs):
        acc = acc + jnp.dot(a_ref[...], w_ref[...], preferred_element_type=jnp.float32)
    if with_norm:
        nw_ref, o_ref, h_ref = refs[2 * n_a + 1:]
        h_ref[...] = (acc * _rms_scale(acc) * nw_ref[...]).astype(h_ref.dtype)
    else:
        o_ref = refs[2 * n_a + 1]
    o_ref[...] = acc


def _res_matmul(a_list, w, res, *, tm, tn, name, norm_w=None):
    m, n = res.shape
    n_a = len(a_list)
    ka = a_list[0].shape[1]
    with_norm = norm_w is not None
    assert not with_norm or tn == n
    in_specs = [pl.BlockSpec((tm, ka), lambda i, j: (i, 0)) for _ in a_list]
    in_specs += [pl.BlockSpec((ka, tn), functools.partial(lambda i, j, kb: (kb, j), kb=kb))
                 for kb in range(n_a)]
    in_specs += [pl.BlockSpec((tm, tn), lambda i, j: (i, j))]
    out_specs = [pl.BlockSpec((tm, tn), lambda i, j: (i, j))]
    out_shape = [jax.ShapeDtypeStruct((m, n), jnp.float32)]
    operands = [*a_list, *([w] * n_a), res]
    if with_norm:
        in_specs += [pl.BlockSpec((1, n), lambda i, j: (0, 0))]
        out_specs += [pl.BlockSpec((tm, tn), lambda i, j: (i, j))]
        out_shape += [jax.ShapeDtypeStruct((m, n), jnp.bfloat16)]
        operands += [norm_w]
    out = pl.pallas_call(
        functools.partial(_res_matmul_kernel, n_a=n_a, with_norm=with_norm),
        grid=(m // tm, n // tn),
        in_specs=in_specs,
        out_specs=out_specs,
        out_shape=out_shape,
        compiler_params=pltpu.CompilerParams(
            dimension_semantics=("parallel", "arbitrary"),
            vmem_limit_bytes=VMEM_LIMIT),
        name=name,
    )(*operands)
    return out if with_norm else out[0]


def _segmented_cumsum(x, row, seg):
    pos = row % seg
    shift = 1
    while shift < seg:
        x = x + jnp.where(pos >= shift, pltpu.roll(x, shift, axis=0), 0.0)
        shift *= 2
    return x


def _hgrn_block(q_ref, f_ref, i_ref, g_ref, lb_ref, nw_ref, o_ref,
                state_ref, qf_ref, qt_ref, k_ref, b_ref, oacc_ref, *, rows):
    n_chunks = rows // CHUNK
    n_sub = CHUNK // SUB

    lbr = lb_ref[...]
    lbe = jnp.exp(lbr - jnp.max(lbr, axis=0, keepdims=True))
    lb = lbe[0:1, :] / jnp.sum(lbe, axis=0, keepdims=True)

    qf = _silu(q_ref[...].astype(jnp.float32))
    f = lb + (1.0 - lb) * (1.0 / (1.0 + jnp.exp(-f_ref[...].astype(jnp.float32))))
    row = lax.broadcasted_iota(jnp.int32, (rows, HG_D), 0)
    b = _segmented_cumsum(jnp.log2(f), row, SUB)
    qf_ref[...] = qf
    qt_ref[...] = qf * jnp.exp2(b)
    k_ref[...] = 1.0 - f
    b_ref[...] = b

    r_i = lax.broadcasted_iota(jnp.int32, (CHUNK, CHUNK), 0)
    c_i = lax.broadcasted_iota(jnp.int32, (CHUNK, CHUNK), 1)
    causal = c_i <= r_i
    off_diag = c_i < (r_i // SUB) * SUB
    lane8 = lax.broadcasted_iota(jnp.int32, (8, CHUNK), 1)
    lane_bits = [(lane8 & (1 << n)) != 0 for n in range(3)]
    mid_block = ((r_i // 8) % 2 == 1) & (c_i // 8 == r_i // 8 - 1)

    def chunk_matmuls(c):
        r0 = c * CHUNK
        sl = pl.ds(r0, CHUNK)
        vc = i_ref[sl, :]
        qt_c = qt_ref[sl, :]

        q_blk, k_blk, b_blk, tot, khat = [], [], [], [], []
        for blk in range(n_sub):
            bs = pl.ds(r0 + blk * SUB, SUB)
            q_blk.append(qf_ref[bs, :])
            k_blk.append(k_ref[bs, :])
            b_blk.append(b_ref[bs, :])
            tot.append(b_ref[pl.ds(r0 + blk * SUB + SUB - 1, 1), :])
            khat.append(k_blk[blk] * jnp.exp2(tot[blk] - b_blk[blk]))

        def span(lo, hi):
            return functools.reduce(lambda a, t: a + t, tot[lo + 1:hi], tot[lo])

        def keys_seen_from(blk):
            parts = [khat[j] * jnp.exp2(span(j + 1, blk)) if j < blk - 1 else khat[j]
                     for j in range(n_sub)]
            return jnp.concatenate(parts, axis=0)

        k_stack = jnp.concatenate([keys_seen_from(blk) for blk in range(1, n_sub)],
                                  axis=0).astype(jnp.bfloat16)
        s_all = lax.dot_general(qt_c.astype(jnp.bfloat16), k_stack, _NT,
                                preferred_element_type=jnp.float32)

        q_mid, k_mid = [], []
        for blk in range(n_sub):
            mid = b_ref[pl.ds(r0 + blk * SUB + 7, 1), :]
            q_mid += [q_blk[blk][0:8, :], q_blk[blk][8:16, :] * jnp.exp2(b_blk[blk][8:16, :] - mid)]
            k_mid += [k_blk[blk][0:8, :] * jnp.exp2(mid - b_blk[blk][0:8, :]), k_blk[blk][8:16, :]]
        s_mid = lax.dot_general(jnp.concatenate(q_mid, axis=0).astype(jnp.bfloat16),
                                jnp.concatenate(k_mid, axis=0).astype(jnp.bfloat16), _NT,
                                preferred_element_type=jnp.float32)

        upd = jnp.dot(vc.astype(jnp.float32).T.astype(jnp.bfloat16),
                      keys_seen_from(n_sub).astype(jnp.bfloat16),
                      preferred_element_type=jnp.float32)

        q_state = jnp.concatenate(
            [qt_c[0:SUB, :]] + [qt_c[blk * SUB:(blk + 1) * SUB, :] * jnp.exp2(span(0, blk))
                                for blk in range(1, n_sub)], axis=0).astype(jnp.bfloat16)
        return dict(r0=r0, sl=sl, vc=vc, q_blk=q_blk, b_blk=b_blk,
                    s_all=s_all, s_mid=s_mid, upd=upd, q_state=q_state,
                    decay=jnp.exp2(span(0, n_sub)))

    def chunk_scores(w):
        r0 = w["r0"]
        off_rows = [jnp.zeros((SUB, CHUNK), jnp.float32)]
        for blk in range(1, n_sub):
            off_rows.append(w["s_all"][blk * SUB:(blk + 1) * SUB, (blk - 1) * CHUNK:blk * CHUNK])
        s_off = jnp.concatenate(off_rows, axis=0)

        diag_rows = []
        for blk in range(n_sub):
            for half in range(SUB // 8):
                q_h = w["q_blk"][blk][half * 8:(half + 1) * 8, :]
                b_h = w["b_blk"][blk][half * 8:(half + 1) * 8, :]
                cols = []
                for s in range(half * 8, (half + 1) * 8):
                    row_s = pl.ds(r0 + blk * SUB + s, 1)
                    d = b_h - b_ref[row_s, :]
                    if s > half * 8:
                        d = jnp.minimum(d, 0.0)
                    cols.append(jnp.sum(q_h * k_ref[row_s, :] * jnp.exp2(d),
                                        axis=-1, keepdims=True))
                bit = 0
                while len(cols) > 1:
                    cols = [jnp.where(lane_bits[bit], cols[n + 1], cols[n])
                            for n in range(0, len(cols), 2)]
                    bit += 1
                diag_rows.append(cols[0])
        s_diag = jnp.concatenate(diag_rows, axis=0)

        return jnp.where(
            causal, jnp.where(off_diag, s_off, jnp.where(mid_block, w["s_mid"], s_diag)),
            0.0).astype(jnp.bfloat16)

    def group_body(g, carry):
        work = [chunk_matmuls(g * GROUP + n) for n in range(GROUP)]
        scores = [chunk_scores(w) for w in work]
        state_t = state_ref[...]
        for w, sc in zip(work, scores):
            o_c = jnp.dot(sc, w["vc"], preferred_element_type=jnp.float32)
            o_c = o_c + lax.dot_general(w["q_state"], state_t.astype(jnp.bfloat16), _NT,
                                        preferred_element_type=jnp.float32)
            oacc_ref[w["sl"], :] = o_c
            state_t = state_t * w["decay"] + w["upd"]
        state_ref[...] = state_t
        return carry

    for g in range(n_chunks // GROUP):
        group_body(g, 0)

    o = oacc_ref[...]
    o = o * _rms_scale(o) * nw_ref[...]
    o_ref[...] = (o * _silu(g_ref[...].astype(jnp.float32))).astype(o_ref.dtype)


def _rope(u, cos_t, sin_t):
    return u * cos_t + pltpu.roll(u, LANE // 2, axis=1) * sin_t


def _rope_table_kernel(pos_ref, invf_ref, cos_ref, sin_ref):
    ang = pos_ref[...].astype(jnp.float32) * invf_ref[...]
    cos_ref[...] = jnp.cos(ang)
    sin_ref[...] = jnp.sin(ang)


def _rope_table(pos_row, invf_col, *, tn):
    m = pos_row.shape[1]
    nf = invf_col.shape[0]
    return pl.pallas_call(
        _rope_table_kernel,
        grid=(m // tn,),
        in_specs=[pl.BlockSpec((1, tn), lambda i: (0, i)),
                  pl.BlockSpec((nf, 1), lambda i: (0, 0))],
        out_specs=[pl.BlockSpec((nf, tn), lambda i: (0, i)),
                   pl.BlockSpec((nf, tn), lambda i: (0, i))],
        out_shape=[jax.ShapeDtypeStruct((nf, m), jnp.float32),
                   jax.ShapeDtypeStruct((nf, m), jnp.float32)],
        compiler_params=pltpu.CompilerParams(dimension_semantics=("parallel",)),
        name="rope_table",
    )(pos_row, invf_col)


def _mla_prep_kernel(cq_ref, ckv_ref, kr_ref, cos_ref, sin_ref, qnw_ref, kvnw_ref, wuq_ref,
                     wukv_ref, qhw_ref, khw_ref, q_ref, k_ref, v_ref):
    cq = cq_ref[...].astype(jnp.float32)
    ckv = ckv_ref[...].astype(jnp.float32)
    cqn = (cq * _rms_scale(cq) * qnw_ref[...]).astype(jnp.bfloat16)
    ckvn = (ckv * _rms_scale(ckv) * kvnw_ref[...]).astype(jnp.bfloat16)

    half = QK_ROPE // 2
    lane = lax.broadcasted_iota(jnp.int32, (1, LANE), 1)
    hi = (lane >= LANE // 2) & (lane < LANE // 2 + half)
    kr_raw = kr_ref[...].astype(jnp.float32)
    kr = jnp.where(lane < half, kr_raw, 0.0) + jnp.where(
        hi, pltpu.roll(kr_raw, LANE // 2 - half, axis=1), 0.0)

    rows = cos_ref.shape[1]
    zpad = jnp.zeros((LANE - half, rows), jnp.float32)
    cos_lo = jnp.concatenate([cos_ref[...], zpad], axis=0).T
    sin_lo = jnp.concatenate([sin_ref[...], zpad], axis=0).T
    cos_t = cos_lo + pltpu.roll(cos_lo, LANE // 2, axis=1)
    sin_t = pltpu.roll(sin_lo, LANE // 2, axis=1) - sin_lo

    qhw = qhw_ref[...]
    khw = khw_ref[...]
    kr_ss = jnp.sum(kr * kr, axis=-1, keepdims=True)
    kr_rot = _rope(kr * khw[:, LANE:QK_PAD], cos_t, sin_t)
    for h in range(MLA_HEADS):
        c0 = h * QK_PAD
        q = jnp.dot(cqn, wuq_ref[:, c0:c0 + QK_PAD], preferred_element_type=jnp.float32)
        qn = q[:, 0:LANE]
        qr = q[:, LANE:QK_PAD]
        ss = jnp.sum(qn * qn + qr * qr, axis=-1, keepdims=True)
        inv = lax.rsqrt(ss * (1.0 / QK_DIM) + NORM_EPS) * (ATTN_SCALE * LOG2_E)
        q_ref[:, c0:c0 + LANE] = (qn * inv * qhw[:, 0:LANE]).astype(q_ref.dtype)
        q_ref[:, c0 + LANE:c0 + QK_PAD] = _rope(qr * inv * qhw[:, LANE:QK_PAD],
                                                cos_t, sin_t).astype(q_ref.dtype)

        kv = jnp.dot(ckvn, wukv_ref[:, c0:c0 + QK_PAD], preferred_element_type=jnp.float32)
        kn = kv[:, 0:LANE]
        ss = jnp.sum(kn * kn, axis=-1, keepdims=True) + kr_ss
        inv = lax.rsqrt(ss * (1.0 / QK_DIM) + NORM_EPS)
        k_ref[:, c0:c0 + LANE] = (kn * inv * khw[:, 0:LANE]).astype(k_ref.dtype)
        k_ref[:, c0 + LANE:c0 + QK_PAD] = (kr_rot * inv).astype(k_ref.dtype)
        v_ref[:, c0:c0 + V_DIM] = kv[:, LANE:QK_PAD].astype(v_ref.dtype)
        v_ref[:, c0 + V_DIM:c0 + 2 * V_DIM] = jnp.ones((v_ref.shape[0], V_DIM), v_ref.dtype)


def _mla_prep(proj, proj_rope, cos_tab, sin_tab, qnw, kvnw, wuq, wukv, qhw, khw, *, tm):
    m = proj.shape[0]
    full = lambda a: pl.BlockSpec(a.shape, lambda i: (0, 0))
    hg_cols = 4 * HG_WIDTH
    nf = cos_tab.shape[0]
    return pl.pallas_call(
        _mla_prep_kernel,
        grid=(m // tm,),
        in_specs=[pl.BlockSpec((tm, Q_LORA), lambda i: (i, hg_cols // Q_LORA)),
                  pl.BlockSpec((tm, KV_LORA), lambda i: (i, (hg_cols + Q_LORA) // KV_LORA)),
                  pl.BlockSpec((tm, LANE), lambda i: (i, 0)),
                  pl.BlockSpec((nf, tm), lambda i: (0, i)),
                  pl.BlockSpec((nf, tm), lambda i: (0, i)),
                  full(qnw), full(kvnw), full(wuq), full(wukv), full(qhw), full(khw)],
        out_specs=[pl.BlockSpec((tm, MLA_HEADS * QK_PAD), lambda i: (i, 0)),
                   pl.BlockSpec((tm, MLA_HEADS * QK_PAD), lambda i: (i, 0)),
                   pl.BlockSpec((tm, MLA_HEADS * 2 * V_DIM), lambda i: (i, 0))],
        out_shape=[jax.ShapeDtypeStruct((m, MLA_HEADS * QK_PAD), jnp.bfloat16),
                   jax.ShapeDtypeStruct((m, MLA_HEADS * QK_PAD), jnp.bfloat16),
                   jax.ShapeDtypeStruct((m, MLA_HEADS * 2 * V_DIM), jnp.bfloat16)],
        compiler_params=pltpu.CompilerParams(
            dimension_semantics=("parallel",),
            vmem_limit_bytes=VMEM_LIMIT),
        name="mla_prep",
    )(proj, proj, proj_rope, cos_tab, sin_tab, qnw, kvnw, wuq, wukv, qhw, khw)


def _mixer_kernel(q_ref, k_ref, v_ref, hq_ref, hf_ref, hi_ref, hg_ref, lb_ref, nw_ref,
                  o_ref, oa_ref, sa_ref, sb_ref, m_ref, acc_ref,
                  state_ref, qf_ref, qt_ref, kk_ref, b_ref, oacc_ref, *, tile, n_tiles, hg_rows):
    n_col = tile // LANE
    half = tile // 2
    t = pl.program_id(2)

    def q_rows(i, start=0, size=tile):
        return pl.ds(pl.multiple_of(i * tile, tile) + start, size)

    def scores(i, j, s_ref):
        ks = pl.ds(pl.multiple_of(j * tile, tile), tile)
        s_ref[...] = lax.dot_general(q_ref[q_rows(i), :], k_ref[ks, :], _NT,
                                     preferred_element_type=jnp.float32)

    def start_tile():
        m_ref[...] = jnp.full_like(m_ref, -jnp.inf)
        acc_ref[...] = jnp.zeros_like(acc_ref)

    def finish_tile(i):
        acc = acc_ref[...]
        o_ref[q_rows(i), :] = (acc[:, :V_DIM] / acc[:, V_DIM:]).astype(o_ref.dtype)

    def diag_scores(i, s_ref):
        for rows, n_keys in ((slice(0, half), half), (slice(half, tile), tile)):
            ks = pl.ds(pl.multiple_of(i * tile, tile), n_keys)
            s_ref[rows, 0:n_keys] = lax.dot_general(
                q_ref[q_rows(i, rows.start, half), :], k_ref[ks, :], _NT,
                preferred_element_type=jnp.float32)

    def hgrn_rows(n):
        view = lambda ref: ref.at[pl.ds(pl.multiple_of(n * hg_rows, hg_rows), hg_rows), :]
        _hgrn_block(view(hq_ref), view(hf_ref), view(hi_ref), view(hg_ref), lb_ref, nw_ref,
                    view(oa_ref), state_ref, qf_ref, qt_ref, kk_ref, b_ref, oacc_ref, rows=hg_rows)

    def softmax_rows(s_ref, rows, n_vis, half_last, n_out):
        cols = [s_ref[rows, c * LANE:(c + 1) * LANE] for c in range(n_vis)]
        if half_last:
            first_chunk = lax.broadcasted_iota(jnp.int32, (1, LANE), 1) < CHUNK
            cols[-1] = jnp.where(first_chunk, cols[-1], -jnp.inf)
        m_prev = m_ref[rows, :]
        m_cur = functools.reduce(jnp.maximum, cols)
        m_new = jnp.maximum(m_prev, jnp.max(m_cur, axis=-1, keepdims=True))
        alpha = jnp.exp2(m_prev - m_new)
        p = [jnp.exp2(s_c - m_new) for s_c in cols]
        p += [jnp.zeros_like(p[0])] * (n_out - n_vis)
        m_ref[rows, :] = m_new
        return jnp.concatenate(p, axis=1).astype(v_ref.dtype), alpha

    def accumulate(rows, alpha, pv):
        acc_ref[rows, :] = jnp.concatenate([alpha, alpha], axis=1) * acc_ref[rows, :] + pv

    def softmax_pv(j, s_ref):
        ks = pl.ds(pl.multiple_of(j * tile, tile), tile)
        p, alpha = softmax_rows(s_ref, slice(None), n_col, False, n_col)
        accumulate(slice(None), alpha,
                   jnp.dot(p, v_ref[ks, :], preferred_element_type=jnp.float32))

    def diag_softmax_pv(i, s_ref):
        lane_chunks = LANE // CHUNK
        for rows, n_keys in ((slice(0, half), half), (slice(half, tile), tile)):
            parts = [softmax_rows(s_ref, slice(c * CHUNK, (c + 1) * CHUNK),
                                  c // lane_chunks + 1, c % lane_chunks != lane_chunks - 1,
                                  n_keys // LANE)
                     for c in range(rows.start // CHUNK, rows.stop // CHUNK)]
            p = jnp.concatenate([pp for pp, _ in parts], axis=0)
            alpha = jnp.concatenate([aa for _, aa in parts], axis=0)
            ks = pl.ds(pl.multiple_of(i * tile, tile), n_keys)
            accumulate(rows, alpha, jnp.dot(p, v_ref[ks, :], preferred_element_type=jnp.float32))

    def query_tile(i, odd, first, other, next_i, hg_first, hg_tail):
        start_tile()

        def pair(jj, carry):
            j = 2 * jj
            scores(i, j + 1, other)
            softmax_pv(j, first)
            scores(i, j + 2, first)
            softmax_pv(j + 1, other)
            hgrn_rows(hg_first + jj)
            return carry

        lax.fori_loop(0, i // 2, pair, 0)
        last, spare = (other, first) if odd else (first, other)
        if odd:
            diag_scores(i, other)
            softmax_pv(i - 1, first)
        if next_i is not None:
            scores(next_i, 0, spare)
        if hg_tail is not None:
            hgrn_rows(hg_tail)
        diag_softmax_pv(i, last)
        finish_tile(i)

    @pl.when(t == 0)
    def _():
        state_ref[...] = jnp.zeros_like(state_ref)

    lo = t
    hi = n_tiles - 1 - t
    hg_last = 2 * tile // hg_rows - 1

    def both_tiles(lo_odd):
        scores(lo, 0, sa_ref)
        query_tile(lo, lo_odd, sa_ref, sb_ref, hi, 0, None)
        hi_first, hi_other = (sa_ref, sb_ref) if lo_odd else (sb_ref, sa_ref)
        query_tile(hi, not lo_odd, hi_first, hi_other, None, lo // 2, hg_last)

    @pl.when(t % 2 == 0)
    def _():
        both_tiles(False)

    @pl.when(t % 2 == 1)
    def _():
        both_tiles(True)


def _mixers(q, k, v1, proj, hgrn_lb, norm_w, *, batch, seq, tile, hg_rows):
    n_tiles = seq // tile
    steps = n_tiles // 2
    rows = seq // steps
    assert n_tiles % 4 == 0 and rows // hg_rows == (n_tiles - 1) // 2 + 1
    vw = 2 * V_DIM
    hb = HG_HEADS

    def hgrn_spec(group):
        return pl.BlockSpec((rows, HG_D), lambda b, h, g: (b * steps + g, group * hb + h))

    scratch = [pltpu.VMEM((tile, tile), jnp.float32),
               pltpu.VMEM((tile, tile), jnp.float32),
               pltpu.VMEM((tile, LANE), jnp.float32),
               pltpu.VMEM((tile, vw), jnp.float32),
               pltpu.VMEM((HG_D, HG_D), jnp.float32)]
    scratch += [pltpu.VMEM((hg_rows, HG_D), jnp.float32) for _ in range(5)]
    out_sds = jax.ShapeDtypeStruct((batch * seq, MLA_HEADS * V_DIM), jnp.bfloat16)
    return pl.pallas_call(
        functools.partial(_mixer_kernel, tile=tile, n_tiles=n_tiles, hg_rows=hg_rows),
        grid=(batch, MLA_HEADS, steps),
        in_specs=[pl.BlockSpec((seq, QK_PAD), lambda b, h, g: (b, h)),
                  pl.BlockSpec((seq, QK_PAD), lambda b, h, g: (b, h)),
                  pl.BlockSpec((seq, vw), lambda b, h, g: (b, h)),
                  hgrn_spec(0), hgrn_spec(1), hgrn_spec(2), hgrn_spec(3),
                  pl.BlockSpec((hgrn_lb.shape[0], HG_D), lambda b, h, g: (0, h)),
                  pl.BlockSpec((1, HG_D), lambda b, h, g: (0, 0))],
        out_specs=[pl.BlockSpec((seq, V_DIM), lambda b, h, g: (b, h)),
                   pl.BlockSpec((rows, V_DIM), lambda b, h, g: (b * steps + g, h))],
        out_shape=[out_sds, out_sds],
        scratch_shapes=scratch,
        compiler_params=pltpu.CompilerParams(
            dimension_semantics=("parallel", "parallel", "arbitrary"),
            vmem_limit_bytes=VMEM_LIMIT),
        name="token_mixers",
    )(q, k, v1, proj, proj, proj, proj, hgrn_lb, norm_w)


def _rope_lanes(a):
    half = QK_ROPE // 2
    z = jnp.zeros(a.shape[:-1] + (LANE // 2 - half,), a.dtype)
    return jnp.concatenate([a[..., :half], z, a[..., half:], z], axis=-1)


def _head_lanes(a):
    return jnp.concatenate([a[..., :QK_NOPE], _rope_lanes(a[..., QK_NOPE:])], axis=-1)


def kernel(x, positions, attn_norm_w, w_in, hgrn_lb, hgrn_norm_w, mla_q_norm_w, w_uq,
           mla_kv_norm_w, w_ukv, q_head_norm_w, k_head_norm_w, w_out, ffn_norm_w,
           w_gate_up, w_down):
    batch, seq, d = x.shape
    m = batch * seq
    bf = jnp.bfloat16
    layer = 0
    xf = x.reshape(m, d)

    hg_cols = 4 * HG_WIDTH
    w_in_l = w_in[layer]
    mla_cols = hg_cols + Q_LORA + KV_LORA
    w_rope = jnp.pad(w_in_l[:, mla_cols:], ((0, 0), (0, LANE - QK_ROPE))).astype(bf)
    wuq = _head_lanes(w_uq[layer].reshape(Q_LORA, MLA_HEADS, QK_DIM)).reshape(
        Q_LORA, MLA_HEADS * QK_PAD).astype(bf)
    wukv = w_ukv[layer].astype(bf)
    qhw = _head_lanes(q_head_norm_w[layer])[None, :]
    khw = _head_lanes(k_head_norm_w[layer])[None, :]
    inv_freq = 1.0 / (ROPE_THETA ** (jnp.arange(0, QK_ROPE, 2, dtype=jnp.float32) / QK_ROPE))
    nw_attn = attn_norm_w[layer][None, :]

    proj, proj_rope = _norm_matmul(xf, nw_attn, w_in_l, w_rope, n_main=mla_cols,
                                   tm=1024, tn=1024, out_dtype=bf, name="in_proj")

    cos_tab, sin_tab = _rope_table(positions.reshape(1, m), inv_freq[:, None], tn=2048)
    q, k, v = _mla_prep(proj, proj_rope, cos_tab, sin_tab,
                        mla_q_norm_w[layer][None, :], mla_kv_norm_w[layer][None, :],
                        wuq, wukv, qhw, khw, tm=512)
    o_b, o_a = _mixers(q, k, v, proj, hgrn_lb, hgrn_norm_w[layer][None, :],
                       batch=batch, seq=seq, tile=1024, hg_rows=GROUP * CHUNK)

    x1, h_ffn = _res_matmul([o_a, o_b], w_out[layer].astype(bf), xf, tm=512, tn=d,
                            name="out_proj", norm_w=ffn_norm_w[layer][None, :])

    act = _swiglu(h_ffn, w_gate_up, layer, tm=2048, tn=512)
    x2 = _res_matmul([act], w_down[layer].astype(bf), x1, tm=1024, tn=512, name="ffn_down")
    return x2.reshape(batch, seq, d)
```

```python
import functools

import jax
import jax.numpy as jnp
import numpy as np
from jax import lax
from jax.experimental import pallas as pl
from jax.experimental.pallas import tpu as pltpu

D_MODEL = 2048
CHUNK = 64
SUB = 16
GROUP = 8
HG_WIDTH = 1024
HG_HEADS = 8
HG_D = 128
MLA_HEADS = 8
V_DIM = 128
QK_NOPE = 128
QK_ROPE = 64
QK_DIM = QK_NOPE + QK_ROPE
QK_PAD = 256
Q_LORA = 512
KV_LORA = 512
ROPE_THETA = 10000.0
ATTN_SCALE = QK_DIM ** -0.5
LOG2_E = 1.4426950408889634
SCORE_BOUND_SLACK = 1.04
MAX_FOLDED_BOUND = 50.0
D_FF = 5632
NORM_EPS = 1e-6
LANE = 128

VMEM_LIMIT = 56 * 1024 * 1024

_NT = (((1,), (1,)), ((), ()))


def _silu(x):
    return x * (1.0 / (1.0 + jnp.exp(-x)))


def _rms_scale(x):
    return lax.rsqrt(jnp.mean(x * x, axis=-1, keepdims=True) + NORM_EPS)


def _norm_matmul_kernel(x_ref, nw_ref, w_ref, wt_ref, o_ref, ot_ref, h_ref):
    @pl.when(pl.program_id(1) == 0)
    def _():
        x = x_ref[...]
        h = (x * _rms_scale(x) * nw_ref[...]).astype(h_ref.dtype)
        h_ref[...] = h
        ot_ref[...] = jnp.dot(h, wt_ref[...], preferred_element_type=jnp.float32).astype(ot_ref.dtype)

    o_ref[...] = jnp.dot(h_ref[...], w_ref[...].astype(h_ref.dtype),
                         preferred_element_type=jnp.float32).astype(o_ref.dtype)


def _norm_matmul(x, nw, w, w_tail, *, n_main, tm, tn, out_dtype, name):
    m, k = x.shape
    nt = w_tail.shape[1]
    return pl.pallas_call(
        _norm_matmul_kernel,
        grid=(m // tm, n_main // tn),
        in_specs=[pl.BlockSpec((tm, k), lambda i, j: (i, 0)),
                  pl.BlockSpec((1, k), lambda i, j: (0, 0)),
                  pl.BlockSpec((k, tn), lambda i, j: (0, j)),
                  pl.BlockSpec((k, nt), lambda i, j: (0, 0))],
        out_specs=[pl.BlockSpec((tm, tn), lambda i, j: (i, j)),
                   pl.BlockSpec((tm, nt), lambda i, j: (i, 0))],
        out_shape=[jax.ShapeDtypeStruct((m, n_main), out_dtype),
                   jax.ShapeDtypeStruct((m, nt), out_dtype)],
        scratch_shapes=[pltpu.VMEM((tm, k), jnp.bfloat16)],
        compiler_params=pltpu.CompilerParams(
            dimension_semantics=("parallel", "arbitrary"),
            vmem_limit_bytes=VMEM_LIMIT),
        name=name,
    )(x, nw, w, w_tail)


def _swiglu_kernel(h_ref, wg_ref, wu_ref, o_ref):
    h = h_ref[...]
    g = jnp.dot(h, wg_ref[...].astype(h.dtype), preferred_element_type=jnp.float32)
    u = jnp.dot(h, wu_ref[...].astype(h.dtype), preferred_element_type=jnp.float32)
    o_ref[...] = (_silu(g) * u).astype(o_ref.dtype)


def _swiglu(h, w_gate_up, layer, *, tm, tn):
    m, k = h.shape
    n = w_gate_up.shape[2] // 2
    up_off = n // tn
    return pl.pallas_call(
        _swiglu_kernel,
        grid=(m // tm, n // tn),
        in_specs=[pl.BlockSpec((tm, k), lambda i, j: (i, 0)),
                  pl.BlockSpec((None, k, tn), lambda i, j: (layer, 0, j)),
                  pl.BlockSpec((None, k, tn), lambda i, j: (layer, 0, j + up_off))],
        out_specs=pl.BlockSpec((tm, tn), lambda i, j: (i, j)),
        out_shape=jax.ShapeDtypeStruct((m, n), jnp.bfloat16),
        compiler_params=pltpu.CompilerParams(
            dimension_semantics=("parallel", "arbitrary"),
            vmem_limit_bytes=VMEM_LIMIT),
        name="ffn_up",
    )(h, w_gate_up, w_gate_up)


def _res_matmul_kernel(*refs, n_a, with_norm):
    a_refs = refs[:n_a]
    w_refs = refs[n_a:2 * n_a]
    r_ref = refs[2 * n_a]
    acc = r_ref[...]
    for a_ref, w_ref in zip(a_refs, w_refs):
        acc = acc + jnp.dot(a_ref[...], w_ref[...], preferred_element_type=jnp.float32)
    if with_norm:
        nw_ref, o_ref, h_ref = refs[2 * n_a + 1:]
        h_ref[...] = (acc * _rms_scale(acc) * nw_ref[...]).astype(h_ref.dtype)
    else:
        o_ref = refs[2 * n_a + 1]
    o_ref[...] = acc


def _res_matmul(a_list, w, res, *, tm, tn, name, norm_w=None):
    m, n = res.shape
    n_a = len(a_list)
    ka = a_list[0].shape[1]
    with_norm = norm_w is not None
    assert not with_norm or tn == n
    in_specs = [pl.BlockSpec((tm, ka), lambda i, j: (i, 0)) for _ in a_list]
    in_specs += [pl.BlockSpec((ka, tn), functools.partial(lambda i, j, kb: (kb, j), kb=kb))
                 for kb in range(n_a)]
    in_specs += [pl.BlockSpec((tm, tn), lambda i, j: (i, j))]
    out_specs = [pl.BlockSpec((tm, tn), lambda i, j: (i, j))]
    out_shape = [jax.ShapeDtypeStruct((m, n), jnp.float32)]
    operands = [*a_list, *([w] * n_a), res]
    if with_norm:
        in_specs += [pl.BlockSpec((1, n), lambda i, j: (0, 0))]
        out_specs += [pl.BlockSpec((tm, tn), lambda i, j: (i, j))]
        out_shape += [jax.ShapeDtypeStruct((m, n), jnp.bfloat16)]
        operands += [norm_w]
    out = pl.pallas_call(
        functools.partial(_res_matmul_kernel, n_a=n_a, with_norm=with_norm),
        grid=(m // tm, n // tn),
        in_specs=in_specs,
        out_specs=out_specs,
        out_shape=out_shape,
        compiler_params=pltpu.CompilerParams(
            dimension_semantics=("parallel", "arbitrary"),
            vmem_limit_bytes=VMEM_LIMIT),
        name=name,
    )(*operands)
    return out if with_norm else out[0]


def _segmented_cumsum(x, row, seg):
    pos = row % seg
    shift = 1
    while shift < seg:
        x = x + jnp.where(pos >= shift, pltpu.roll(x, shift, axis=0), 0.0)
        shift *= 2
    return x


def _hgrn_block(q_ref, f_ref, i_ref, g_ref, lb_ref, nw_ref, o_ref,
                state_ref, qf_ref, qt_ref, k_ref, b_ref, oacc_ref, *, rows):
    n_chunks = rows // CHUNK
    n_sub = CHUNK // SUB

    lbr = lb_ref[...]
    lbe = jnp.exp(lbr - jnp.max(lbr, axis=0, keepdims=True))
    lb = lbe[0:1, :] / jnp.sum(lbe, axis=0, keepdims=True)

    qf = _silu(q_ref[...].astype(jnp.float32))
    f = lb + (1.0 - lb) * (1.0 / (1.0 + jnp.exp(-f_ref[...].astype(jnp.float32))))
    row = lax.broadcasted_iota(jnp.int32, (rows, HG_D), 0)
    b = _segmented_cumsum(jnp.log2(f), row, SUB)
    qf_ref[...] = qf
    qt_ref[...] = qf * jnp.exp2(b)
    k_ref[...] = 1.0 - f
    b_ref[...] = b

    r_i = lax.broadcasted_iota(jnp.int32, (CHUNK, CHUNK), 0)
    c_i = lax.broadcasted_iota(jnp.int32, (CHUNK, CHUNK), 1)
    causal = c_i <= r_i
    off_diag = c_i < (r_i // SUB) * SUB
    lane8 = lax.broadcasted_iota(jnp.int32, (8, CHUNK), 1)
    lane_bits = [(lane8 & (1 << n)) != 0 for n in range(3)]
    mid_block = ((r_i // 8) % 2 == 1) & (c_i // 8 == r_i // 8 - 1)

    def chunk_matmuls(c):
        r0 = c * CHUNK
        sl = pl.ds(r0, CHUNK)
        vc = i_ref[sl, :]
        qt_c = qt_ref[sl, :]

        q_blk, k_blk, b_blk, tot, khat = [], [], [], [], []
        for blk in range(n_sub):
            bs = pl.ds(r0 + blk * SUB, SUB)
            q_blk.append(qf_ref[bs, :])
            k_blk.append(k_ref[bs, :])
            b_blk.append(b_ref[bs, :])
            tot.append(b_ref[pl.ds(r0 + blk * SUB + SUB - 1, 1), :])
            khat.append(k_blk[blk] * jnp.exp2(tot[blk] - b_blk[blk]))

        def span(lo, hi):
            return functools.reduce(lambda a, t: a + t, tot[lo + 1:hi], tot[lo])

        def keys_seen_from(blk):
            parts = [khat[j] * jnp.exp2(span(j + 1, blk)) if j < blk - 1 else khat[j]
                     for j in range(n_sub)]
            return jnp.concatenate(parts, axis=0)

        k_stack = jnp.concatenate([keys_seen_from(blk) for blk in range(1, n_sub)],
                                  axis=0).astype(jnp.bfloat16)
        s_all = lax.dot_general(qt_c.astype(jnp.bfloat16), k_stack, _NT,
                                preferred_element_type=jnp.float32)

        q_mid, k_mid = [], []
        for blk in range(n_sub):
            mid = b_ref[pl.ds(r0 + blk * SUB + 7, 1), :]
            q_mid += [q_blk[blk][0:8, :], q_blk[blk][8:16, :] * jnp.exp2(b_blk[blk][8:16, :] - mid)]
            k_mid += [k_blk[blk][0:8, :] * jnp.exp2(mid - b_blk[blk][0:8, :]), k_blk[blk][8:16, :]]
        s_mid = lax.dot_general(jnp.concatenate(q_mid, axis=0).astype(jnp.bfloat16),
                                jnp.concatenate(k_mid, axis=0).astype(jnp.bfloat16), _NT,
                                preferred_element_type=jnp.float32)

        upd = jnp.dot(vc.astype(jnp.float32).T.astype(jnp.bfloat16),
                      keys_seen_from(n_sub).astype(jnp.bfloat16),
                      preferred_element_type=jnp.float32)

        q_state = jnp.concatenate(
            [qt_c[0:SUB, :]] + [qt_c[blk * SUB:(blk + 1) * SUB, :] * jnp.exp2(span(0, blk))
                                for blk in range(1, n_sub)], axis=0).astype(jnp.bfloat16)
        return dict(r0=r0, sl=sl, vc=vc, q_blk=q_blk, b_blk=b_blk,
                    s_all=s_all, s_mid=s_mid, upd=upd, q_state=q_state,
                    decay=jnp.exp2(span(0, n_sub)))

    def chunk_scores(w):
        r0 = w["r0"]
        off_rows = [jnp.zeros((SUB, CHUNK), jnp.float32)]
        for blk in range(1, n_sub):
            off_rows.append(w["s_all"][blk * SUB:(blk + 1) * SUB, (blk - 1) * CHUNK:blk * CHUNK])
        s_off = jnp.concatenate(off_rows, axis=0)

        diag_rows = []
        for blk in range(n_sub):
            for half in range(SUB // 8):
                q_h = w["q_blk"][blk][half * 8:(half + 1) * 8, :]
                b_h = w["b_blk"][blk][half * 8:(half + 1) * 8, :]
                cols = []
                for s in range(half * 8, (half + 1) * 8):
                    row_s = pl.ds(r0 + blk * SUB + s, 1)
                    d = b_h - b_ref[row_s, :]
                    if s > half * 8:
                        d = jnp.minimum(d, 0.0)
                    cols.append(jnp.sum(q_h * k_ref[row_s, :] * jnp.exp2(d),
                                        axis=-1, keepdims=True))
                bit = 0
                while len(cols) > 1:
                    cols = [jnp.where(lane_bits[bit], cols[n + 1], cols[n])
                            for n in range(0, len(cols), 2)]
                    bit += 1
                diag_rows.append(cols[0])
        s_diag = jnp.concatenate(diag_rows, axis=0)

        return jnp.where(
            causal, jnp.where(off_diag, s_off, jnp.where(mid_block, w["s_mid"], s_diag)),
            0.0).astype(jnp.bfloat16)

    def group_body(g, carry):
        work = [chunk_matmuls(g * GROUP + n) for n in range(GROUP)]
        scores = [chunk_scores(w) for w in work]
        state_t = state_ref[...]
        for w, sc in zip(work, scores):
            o_c = jnp.dot(sc, w["vc"], preferred_element_type=jnp.float32)
            o_c = o_c + lax.dot_general(w["q_state"], state_t.astype(jnp.bfloat16), _NT,
                                        preferred_element_type=jnp.float32)
            oacc_ref[w["sl"], :] = o_c
            state_t = state_t * w["decay"] + w["upd"]
        state_ref[...] = state_t
        return carry

    for g in range(n_chunks // GROUP):
        group_body(g, 0)

    o = oacc_ref[...]
    o = o * _rms_scale(o) * nw_ref[...]
    o_ref[...] = (o * _silu(g_ref[...].astype(jnp.float32))).astype(o_ref.dtype)


def _rope(u, cos_t, sin_t):
    return u * cos_t + pltpu.roll(u, LANE // 2, axis=1) * sin_t


def _rope_table_kernel(pos_ref, invf_ref, cos_ref, sin_ref):
    ang = pos_ref[...].astype(jnp.float32) * invf_ref[...]
    cos_ref[...] = jnp.cos(ang)
    sin_ref[...] = jnp.sin(ang)


def _rope_table(pos_row, invf_col, *, tn):
    m = pos_row.shape[1]
    nf = invf_col.shape[0]
    return pl.pallas_call(
        _rope_table_kernel,
        grid=(m // tn,),
        in_specs=[pl.BlockSpec((1, tn), lambda i: (0, i)),
                  pl.BlockSpec((nf, 1), lambda i: (0, 0))],
        out_specs=[pl.BlockSpec((nf, tn), lambda i: (0, i)),
                   pl.BlockSpec((nf, tn), lambda i: (0, i))],
        out_shape=[jax.ShapeDtypeStruct((nf, m), jnp.float32),
                   jax.ShapeDtypeStruct((nf, m), jnp.float32)],
        compiler_params=pltpu.CompilerParams(dimension_semantics=("parallel",)),
        name="rope_table",
    )(pos_row, invf_col)


def _mla_prep_kernel(cq_ref, ckv_ref, kr_ref, cos_ref, sin_ref, qnw_ref, kvnw_ref, wuq_ref,
                     wukv_ref, qhw_ref, khw_ref, const_ref, q_ref, k_ref, v_ref):
    q_const = const_ref[0:1, :]
    k_const = const_ref[1:2, :]
    cq = cq_ref[...].astype(jnp.float32)
    ckv = ckv_ref[...].astype(jnp.float32)
    cqn = (cq * _rms_scale(cq) * qnw_ref[...]).astype(jnp.bfloat16)
    ckvn = (ckv * _rms_scale(ckv) * kvnw_ref[...]).astype(jnp.bfloat16)

    half = QK_ROPE // 2
    lane = lax.broadcasted_iota(jnp.int32, (1, LANE), 1)
    hi = (lane >= LANE // 2) & (lane < LANE // 2 + half)
    kr_raw = kr_ref[...].astype(jnp.float32)
    kr = jnp.where(lane < half, kr_raw, 0.0) + jnp.where(
        hi, pltpu.roll(kr_raw, LANE // 2 - half, axis=1), 0.0)

    rows = cos_ref.shape[1]
    zpad = jnp.zeros((LANE - half, rows), jnp.float32)
    cos_lo = jnp.concatenate([cos_ref[...], zpad], axis=0).T
    sin_lo = jnp.concatenate([sin_ref[...], zpad], axis=0).T
    cos_t = cos_lo + pltpu.roll(cos_lo, LANE // 2, axis=1)
    sin_t = pltpu.roll(sin_lo, LANE // 2, axis=1) - sin_lo

    qhw = qhw_ref[...]
    khw = khw_ref[...]
    kr_ss = jnp.sum(kr * kr, axis=-1, keepdims=True)
    kr_rot = _rope(kr * khw[:, LANE:QK_PAD], cos_t, sin_t)
    for h in range(MLA_HEADS):
        c0 = h * QK_PAD
        q = jnp.dot(cqn, wuq_ref[:, c0:c0 + QK_PAD], preferred_element_type=jnp.float32)
        qn = q[:, 0:LANE]
        qr = q[:, LANE:QK_PAD]
        ss = jnp.sum(qn * qn + qr * qr, axis=-1, keepdims=True)
        inv = lax.rsqrt(ss * (1.0 / QK_DIM) + NORM_EPS) * (ATTN_SCALE * LOG2_E)
        q_ref[:, c0:c0 + LANE] = (qn * inv * qhw[:, 0:LANE]).astype(q_ref.dtype)
        q_ref[:, c0 + LANE:c0 + QK_PAD] = (_rope(qr * inv * qhw[:, LANE:QK_PAD], cos_t, sin_t)
                                           + q_const).astype(q_ref.dtype)

        kv = jnp.dot(ckvn, wukv_ref[:, c0:c0 + QK_PAD], preferred_element_type=jnp.float32)
        kn = kv[:, 0:LANE]
        ss = jnp.sum(kn * kn, axis=-1, keepdims=True) + kr_ss
        inv = lax.rsqrt(ss * (1.0 / QK_DIM) + NORM_EPS)
        k_ref[:, c0:c0 + LANE] = (kn * inv * khw[:, 0:LANE]).astype(k_ref.dtype)
        k_ref[:, c0 + LANE:c0 + QK_PAD] = (kr_rot * inv + k_const).astype(k_ref.dtype)
        v_ref[:, c0:c0 + V_DIM] = kv[:, LANE:QK_PAD].astype(v_ref.dtype)
        v_ref[:, c0 + V_DIM:c0 + 2 * V_DIM] = jnp.ones((v_ref.shape[0], V_DIM), v_ref.dtype)


def _mla_prep(proj, proj_rope, cos_tab, sin_tab, qnw, kvnw, wuq, wukv, qhw, khw, lane_consts,
              *, tm):
    m = proj.shape[0]
    full = lambda a: pl.BlockSpec(a.shape, lambda i: (0, 0))
    hg_cols = 4 * HG_WIDTH
    nf = cos_tab.shape[0]
    return pl.pallas_call(
        _mla_prep_kernel,
        grid=(m // tm,),
        in_specs=[pl.BlockSpec((tm, Q_LORA), lambda i: (i, hg_cols // Q_LORA)),
                  pl.BlockSpec((tm, KV_LORA), lambda i: (i, (hg_cols + Q_LORA) // KV_LORA)),
                  pl.BlockSpec((tm, LANE), lambda i: (i, 0)),
                  pl.BlockSpec((nf, tm), lambda i: (0, i)),
                  pl.BlockSpec((nf, tm), lambda i: (0, i)),
                  full(qnw), full(kvnw), full(wuq), full(wukv), full(qhw), full(khw),
                  full(lane_consts)],
        out_specs=[pl.BlockSpec((tm, MLA_HEADS * QK_PAD), lambda i: (i, 0)),
                   pl.BlockSpec((tm, MLA_HEADS * QK_PAD), lambda i: (i, 0)),
                   pl.BlockSpec((tm, MLA_HEADS * 2 * V_DIM), lambda i: (i, 0))],
        out_shape=[jax.ShapeDtypeStruct((m, MLA_HEADS * QK_PAD), jnp.bfloat16),
                   jax.ShapeDtypeStruct((m, MLA_HEADS * QK_PAD), jnp.bfloat16),
                   jax.ShapeDtypeStruct((m, MLA_HEADS * 2 * V_DIM), jnp.bfloat16)],
        compiler_params=pltpu.CompilerParams(
            dimension_semantics=("parallel",),
            vmem_limit_bytes=VMEM_LIMIT),
        name="mla_prep",
    )(proj, proj, proj_rope, cos_tab, sin_tab, qnw, kvnw, wuq, wukv, qhw, khw, lane_consts)


def _mixer_kernel(flag_ref, q_ref, k_ref, v_ref, hq_ref, hf_ref, hi_ref, hg_ref, lb_ref, nw_ref,
                  o_ref, oa_ref, sa_ref, sb_ref, m_ref, acc_ref,
                  state_ref, qf_ref, qt_ref, kk_ref, b_ref, oacc_ref, *, tile, n_tiles, hg_rows):
    n_col = tile // LANE
    half = tile // 2
    t = pl.program_id(2)
    bounded = [False]

    def q_rows(i, start=0, size=tile):
        return pl.ds(pl.multiple_of(i * tile, tile) + start, size)

    def scores(i, j, s_ref):
        ks = pl.ds(pl.multiple_of(j * tile, tile), tile)
        s_ref[...] = lax.dot_general(q_ref[q_rows(i), :], k_ref[ks, :], _NT,
                                     preferred_element_type=jnp.float32)

    def start_tile():
        m_ref[...] = jnp.full_like(m_ref, -jnp.inf)
        acc_ref[...] = jnp.zeros_like(acc_ref)

    def finish_tile(i):
        acc = acc_ref[...]
        o_ref[q_rows(i), :] = (acc[:, :V_DIM] / acc[:, V_DIM:]).astype(o_ref.dtype)

    def diag_scores(i, s_ref):
        for rows, n_keys in ((slice(0, half), half), (slice(half, tile), tile)):
            ks = pl.ds(pl.multiple_of(i * tile, tile), n_keys)
            s_ref[rows, 0:n_keys] = lax.dot_general(
                q_ref[q_rows(i, rows.start, half), :], k_ref[ks, :], _NT,
                preferred_element_type=jnp.float32)

    def hgrn_rows(n):
        view = lambda ref: ref.at[pl.ds(pl.multiple_of(n * hg_rows, hg_rows), hg_rows), :]
        _hgrn_block(view(hq_ref), view(hf_ref), view(hi_ref), view(hg_ref), lb_ref, nw_ref,
                    view(oa_ref), state_ref, qf_ref, qt_ref, kk_ref, b_ref, oacc_ref, rows=hg_rows)

    def softmax_rows(s_ref, rows, n_vis, half_last, n_out):
        cols = [s_ref[rows, c * LANE:(c + 1) * LANE] for c in range(n_vis)]
        if half_last:
            first_chunk = lax.broadcasted_iota(jnp.int32, (1, LANE), 1) < CHUNK
            cols[-1] = jnp.where(first_chunk, cols[-1], -jnp.inf)
        if bounded[0]:
            alpha = None
            p = [jnp.exp2(s_c) for s_c in cols]
        else:
            m_prev = m_ref[rows, :]
            m_cur = functools.reduce(jnp.maximum, cols)
            m_new = jnp.maximum(m_prev, jnp.max(m_cur, axis=-1, keepdims=True))
            alpha = jnp.exp2(m_prev - m_new)
            p = [jnp.exp2(s_c - m_new) for s_c in cols]
            m_ref[rows, :] = m_new
        p += [jnp.zeros_like(p[0])] * (n_out - n_vis)
        return jnp.concatenate(p, axis=1).astype(v_ref.dtype), alpha

    def accumulate(rows, alpha, pv):
        if alpha is None:
            acc_ref[rows, :] = acc_ref[rows, :] + pv
        else:
            acc_ref[rows, :] = jnp.concatenate([alpha, alpha], axis=1) * acc_ref[rows, :] + pv

    def softmax_pv(j, s_ref):
        ks = pl.ds(pl.multiple_of(j * tile, tile), tile)
        p, alpha = softmax_rows(s_ref, slice(None), n_col, False, n_col)
        accumulate(slice(None), alpha,
                   jnp.dot(p, v_ref[ks, :], preferred_element_type=jnp.float32))

    def diag_softmax_pv(i, s_ref):
        lane_chunks = LANE // CHUNK
        for rows, n_keys in ((slice(0, half), half), (slice(half, tile), tile)):
            parts = [softmax_rows(s_ref, slice(c * CHUNK, (c + 1) * CHUNK),
                                  c // lane_chunks + 1, c % lane_chunks != lane_chunks - 1,
                                  n_keys // LANE)
                     for c in range(rows.start // CHUNK, rows.stop // CHUNK)]
            p = jnp.concatenate([pp for pp, _ in parts], axis=0)
            alpha = None if bounded[0] else jnp.concatenate([aa for _, aa in parts], axis=0)
            ks = pl.ds(pl.multiple_of(i * tile, tile), n_keys)
            accumulate(rows, alpha, jnp.dot(p, v_ref[ks, :], preferred_element_type=jnp.float32))

    def query_tile(i, odd, first, other, next_i, hg_first, hg_tail):
        start_tile()

        def pair(jj, carry):
            j = 2 * jj
            scores(i, j + 1, other)
            softmax_pv(j, first)
            scores(i, j + 2, first)
            softmax_pv(j + 1, other)
            hgrn_rows(hg_first + jj)
            return carry

        lax.fori_loop(0, i // 2, pair, 0)
        last, spare = (other, first) if odd else (first, other)
        if odd:
            diag_scores(i, other)
            softmax_pv(i - 1, first)
        if next_i is not None:
            scores(next_i, 0, spare)
        if hg_tail is not None:
            hgrn_rows(hg_tail)
        diag_softmax_pv(i, last)
        finish_tile(i)

    @pl.when(t == 0)
    def _():
        state_ref[...] = jnp.zeros_like(state_ref)

    lo = t
    hi = n_tiles - 1 - t
    hg_last = 2 * tile // hg_rows - 1

    def both_tiles(lo_odd, scores_bounded):
        bounded[0] = scores_bounded
        scores(lo, 0, sa_ref)
        query_tile(lo, lo_odd, sa_ref, sb_ref, hi, 0, None)
        hi_first, hi_other = (sa_ref, sb_ref) if lo_odd else (sb_ref, sa_ref)
        query_tile(hi, not lo_odd, hi_first, hi_other, None, lo // 2, hg_last)

    for scores_bounded in (True, False):
        for lo_odd in (False, True):
            @pl.when((t % 2 == int(lo_odd)) & (flag_ref[0] == int(scores_bounded)))
            def _():
                both_tiles(lo_odd, scores_bounded)


def _mixers(bounded_flag, q, k, v1, proj, hgrn_lb, norm_w, *, batch, seq, tile, hg_rows):
    n_tiles = seq // tile
    steps = n_tiles // 2
    rows = seq // steps
    assert n_tiles % 4 == 0 and rows // hg_rows == (n_tiles - 1) // 2 + 1
    vw = 2 * V_DIM
    hb = HG_HEADS

    def hgrn_spec(group):
        return pl.BlockSpec((rows, HG_D), lambda b, h, g: (b * steps + g, group * hb + h))

    scratch = [pltpu.VMEM((tile, tile), jnp.float32),
               pltpu.VMEM((tile, tile), jnp.float32),
               pltpu.VMEM((tile, LANE), jnp.float32),
               pltpu.VMEM((tile, vw), jnp.float32),
               pltpu.VMEM((HG_D, HG_D), jnp.float32)]
    scratch += [pltpu.VMEM((hg_rows, HG_D), jnp.float32) for _ in range(5)]
    out_sds = jax.ShapeDtypeStruct((batch * seq, MLA_HEADS * V_DIM), jnp.bfloat16)
    return pl.pallas_call(
        functools.partial(_mixer_kernel, tile=tile, n_tiles=n_tiles, hg_rows=hg_rows),
        grid=(batch, MLA_HEADS, steps),
        in_specs=[pl.BlockSpec(memory_space=pltpu.SMEM),
                  pl.BlockSpec((seq, QK_PAD), lambda b, h, g: (b, h)),
                  pl.BlockSpec((seq, QK_PAD), lambda b, h, g: (b, h)),
                  pl.BlockSpec((seq, vw), lambda b, h, g: (b, h)),
                  hgrn_spec(0), hgrn_spec(1), hgrn_spec(2), hgrn_spec(3),
                  pl.BlockSpec((hgrn_lb.shape[0], HG_D), lambda b, h, g: (0, h)),
                  pl.BlockSpec((1, HG_D), lambda b, h, g: (0, 0))],
        out_specs=[pl.BlockSpec((seq, V_DIM), lambda b, h, g: (b, h)),
                   pl.BlockSpec((rows, V_DIM), lambda b, h, g: (b * steps + g, h))],
        out_shape=[out_sds, out_sds],
        scratch_shapes=scratch,
        compiler_params=pltpu.CompilerParams(
            dimension_semantics=("parallel", "parallel", "arbitrary"),
            vmem_limit_bytes=VMEM_LIMIT),
        name="token_mixers",
    )(bounded_flag, q, k, v1, proj, proj, proj, proj, hgrn_lb, norm_w)


def _rope_lanes(a):
    half = QK_ROPE // 2
    z = jnp.zeros(a.shape[:-1] + (LANE // 2 - half,), a.dtype)
    return jnp.concatenate([a[..., :half], z, a[..., half:], z], axis=-1)


def _head_lanes(a):
    return jnp.concatenate([a[..., :QK_NOPE], _rope_lanes(a[..., QK_NOPE:])], axis=-1)


def kernel(x, positions, attn_norm_w, w_in, hgrn_lb, hgrn_norm_w, mla_q_norm_w, w_uq,
           mla_kv_norm_w, w_ukv, q_head_norm_w, k_head_norm_w, w_out, ffn_norm_w,
           w_gate_up, w_down):
    batch, seq, d = x.shape
    m = batch * seq
    bf = jnp.bfloat16
    layer = 0
    xf = x.reshape(m, d)

    hg_cols = 4 * HG_WIDTH
    w_in_l = w_in[layer]
    mla_cols = hg_cols + Q_LORA + KV_LORA
    w_rope = jnp.pad(w_in_l[:, mla_cols:], ((0, 0), (0, LANE - QK_ROPE))).astype(bf)
    wuq = _head_lanes(w_uq[layer].reshape(Q_LORA, MLA_HEADS, QK_DIM)).reshape(
        Q_LORA, MLA_HEADS * QK_PAD).astype(bf)
    wukv = w_ukv[layer].astype(bf)
    qhw = _head_lanes(q_head_norm_w[layer])[None, :]
    khw = _head_lanes(k_head_norm_w[layer])[None, :]
    inv_freq = 1.0 / (ROPE_THETA ** (jnp.arange(0, QK_ROPE, 2, dtype=jnp.float32) / QK_ROPE))
    nw_attn = attn_norm_w[layer][None, :]

    score_bound = (QK_DIM * ATTN_SCALE * LOG2_E * SCORE_BOUND_SLACK
                   * jnp.max(jnp.abs(q_head_norm_w[layer])) * jnp.max(jnp.abs(k_head_norm_w[layer]))
                   ).astype(bf).astype(jnp.float32)
    last_lane = (jnp.arange(LANE) == LANE - 1).astype(jnp.float32)
    lane_consts = jnp.stack([last_lane, -score_bound * last_lane])
    bounded_flag = (score_bound <= MAX_FOLDED_BOUND).astype(jnp.int32).reshape(1)

    proj, proj_rope = _norm_matmul(xf, nw_attn, w_in_l, w_rope, n_main=mla_cols,
                                   tm=1024, tn=1024, out_dtype=bf, name="in_proj")

    cos_tab, sin_tab = _rope_table(positions.reshape(1, m), inv_freq[:, None], tn=2048)
    q, k, v = _mla_prep(proj, proj_rope, cos_tab, sin_tab,
                        mla_q_norm_w[layer][None, :], mla_kv_norm_w[layer][None, :],
                        wuq, wukv, qhw, khw, lane_consts, tm=1024)
    o_b, o_a = _mixers(bounded_flag, q, k, v, proj, hgrn_lb, hgrn_norm_w[layer][None, :],
                       batch=batch, seq=seq, tile=1024, hg_rows=GROUP * CHUNK)

    x1, h_ffn = _res_matmul([o_a, o_b], w_out[layer].astype(bf), xf, tm=512, tn=d,
                            name="out_proj", norm_w=ffn_norm_w[layer][None, :])

    act = _swiglu(h_ffn, w_gate_up, layer, tm=2048, tn=512)
    x2 = _res_matmul([act], w_down[layer].astype(bf), x1, tm=1024, tn=512, name="ffn_down")
    return x2.reshape(batch, seq, d)
```

```python
import functools

import jax
import jax.numpy as jnp
import numpy as np
from jax import lax
from jax.experimental import pallas as pl
from jax.experimental.pallas import tpu as pltpu

D_MODEL = 2048
CHUNK = 64
SUB = 16
GROUP = 8
HG_WIDTH = 1024
HG_HEADS = 8
HG_D = 128
MLA_HEADS = 8
V_DIM = 128
QK_NOPE = 128
QK_ROPE = 64
QK_DIM = QK_NOPE + QK_ROPE
QK_PAD = 256
Q_LORA = 512
KV_LORA = 512
ROPE_THETA = 10000.0
ATTN_SCALE = QK_DIM ** -0.5
LOG2_E = 1.4426950408889634
SCORE_BOUND_SLACK = 1.04
MAX_FOLDED_BOUND = 50.0
D_FF = 5632
NORM_EPS = 1e-6
LANE = 128

VMEM_LIMIT = 56 * 1024 * 1024

_NT = (((1,), (1,)), ((), ()))


def _silu(x):
    return x * (1.0 / (1.0 + jnp.exp(-x)))


def _rms_scale(x):
    return lax.rsqrt(jnp.mean(x * x, axis=-1, keepdims=True) + NORM_EPS)


def _norm_matmul_kernel(x_ref, nw_ref, w_ref, wt_ref, o_ref, ot_ref, h_ref):
    @pl.when(pl.program_id(1) == 0)
    def _():
        x = x_ref[...]
        h = (x * _rms_scale(x) * nw_ref[...]).astype(h_ref.dtype)
        h_ref[...] = h
        ot_ref[...] = jnp.dot(h, wt_ref[...], preferred_element_type=jnp.float32).astype(ot_ref.dtype)

    o_ref[...] = jnp.dot(h_ref[...], w_ref[...].astype(h_ref.dtype),
                         preferred_element_type=jnp.float32).astype(o_ref.dtype)


def _norm_matmul(x, nw, w, w_tail, *, n_main, tm, tn, out_dtype, name):
    m, k = x.shape
    nt = w_tail.shape[1]
    return pl.pallas_call(
        _norm_matmul_kernel,
        grid=(m // tm, n_main // tn),
        in_specs=[pl.BlockSpec((tm, k), lambda i, j: (i, 0)),
                  pl.BlockSpec((1, k), lambda i, j: (0, 0)),
                  pl.BlockSpec((k, tn), lambda i, j: (0, j)),
                  pl.BlockSpec((k, nt), lambda i, j: (0, 0))],
        out_specs=[pl.BlockSpec((tm, tn), lambda i, j: (i, j)),
                   pl.BlockSpec((tm, nt), lambda i, j: (i, 0))],
        out_shape=[jax.ShapeDtypeStruct((m, n_main), out_dtype),
                   jax.ShapeDtypeStruct((m, nt), out_dtype)],
        scratch_shapes=[pltpu.VMEM((tm, k), jnp.bfloat16)],
        compiler_params=pltpu.CompilerParams(
            dimension_semantics=("parallel", "arbitrary"),
            vmem_limit_bytes=VMEM_LIMIT),
        name=name,
    )(x, nw, w, w_tail)


def _swiglu_kernel(h_ref, wg_ref, wu_ref, o_ref):
    h = h_ref[...]
    g = jnp.dot(h, wg_ref[...].astype(h.dtype), preferred_element_type=jnp.float32)
    u = jnp.dot(h, wu_ref[...].astype(h.dtype), preferred_element_type=jnp.float32)
    o_ref[...] = (_silu(g) * u).astype(o_ref.dtype)


def _swiglu(h, w_gate_up, layer, *, tm, tn):
    m, k = h.shape
    n = w_gate_up.shape[2] // 2
    up_off = n // tn
    return pl.pallas_call(
        _swiglu_kernel,
        grid=(m // tm, n // tn),
        in_specs=[pl.BlockSpec((tm, k), lambda i, j: (i, 0)),
                  pl.BlockSpec((None, k, tn), lambda i, j: (layer, 0, j)),
                  pl.BlockSpec((None, k, tn), lambda i, j: (layer, 0, j + up_off))],
        out_specs=pl.BlockSpec((tm, tn), lambda i, j: (i, j)),
        out_shape=jax.ShapeDtypeStruct((m, n), jnp.bfloat16),
        compiler_params=pltpu.CompilerParams(
            dimension_semantics=("parallel", "arbitrary"),
            vmem_limit_bytes=VMEM_LIMIT),
        name="ffn_up",
    )(h, w_gate_up, w_gate_up)


def _res_matmul_kernel(*refs, n_a, with_norm):
    a_refs = refs[:n_a]
    w_refs = refs[n_a:2 * n_a]
    r_ref = refs[2 * n_a]
    acc = r_ref[...]
    for a_ref, w_ref in zip(a_refs, w_refs):
        acc = acc + jnp.dot(a_ref[...], w_ref[...], preferred_element_type=jnp.float32)
    if with_norm:
        nw_ref, o_ref, h_ref = refs[2 * n_a + 1:]
        h_ref[...] = (acc * _rms_scale(acc) * nw_ref[...]).astype(h_ref.dtype)
    else:
        o_ref = refs[2 * n_a + 1]
    o_ref[...] = acc


def _res_matmul(a_list, w, res, *, tm, tn, name, norm_w=None):
    m, n = res.shape
    n_a = len(a_list)
    ka = a_list[0].shape[1]
    with_norm = norm_w is not None
    assert not with_norm or tn == n
    in_specs = [pl.BlockSpec((tm, ka), lambda i, j: (i, 0)) for _ in a_list]
    in_specs += [pl.BlockSpec((ka, tn), functools.partial(lambda i, j, kb: (kb, j), kb=kb))
                 for kb in range(n_a)]
    in_specs += [pl.BlockSpec((tm, tn), lambda i, j: (i, j))]
    out_specs = [pl.BlockSpec((tm, tn), lambda i, j: (i, j))]
    out_shape = [jax.ShapeDtypeStruct((m, n), jnp.float32)]
    operands = [*a_list, *([w] * n_a), res]
    if with_norm:
        in_specs += [pl.BlockSpec((1, n), lambda i, j: (0, 0))]
        out_specs += [pl.BlockSpec((tm, tn), lambda i, j: (i, j))]
        out_shape += [jax.ShapeDtypeStruct((m, n), jnp.bfloat16)]
        operands += [norm_w]
    out = pl.pallas_call(
        functools.partial(_res_matmul_kernel, n_a=n_a, with_norm=with_norm),
        grid=(m // tm, n // tn),
        in_specs=in_specs,
        out_specs=out_specs,
        out_shape=out_shape,
        compiler_params=pltpu.CompilerParams(
            dimension_semantics=("parallel", "arbitrary"),
            vmem_limit_bytes=VMEM_LIMIT),
        name=name,
    )(*operands)
    return out if with_norm else out[0]


def _segmented_cumsum(x, row, seg):
    pos = row % seg
    shift = 1
    while shift < seg:
        x = x + jnp.where(pos >= shift, pltpu.roll(x, shift, axis=0), 0.0)
        shift *= 2
    return x


def _hgrn_block(q_ref, f_ref, i_ref, g_ref, lb_ref, nw_ref, o_ref,
                state_ref, qf_ref, qt_ref, k_ref, b_ref, oacc_ref, *, rows):
    n_chunks = rows // CHUNK
    n_sub = CHUNK // SUB

    lbr = lb_ref[...]
    lbe = jnp.exp(lbr - jnp.max(lbr, axis=0, keepdims=True))
    lb = lbe[0:1, :] / jnp.sum(lbe, axis=0, keepdims=True)

    qf = _silu(q_ref[...].astype(jnp.float32))
    f = lb + (1.0 - lb) * (1.0 / (1.0 + jnp.exp(-f_ref[...].astype(jnp.float32))))
    row = lax.broadcasted_iota(jnp.int32, (rows, HG_D), 0)
    b = _segmented_cumsum(jnp.log2(f), row, SUB)
    qf_ref[...] = qf
    qt_ref[...] = qf * jnp.exp2(b)
    k_ref[...] = 1.0 - f
    b_ref[...] = b

    r_i = lax.broadcasted_iota(jnp.int32, (CHUNK, CHUNK), 0)
    c_i = lax.broadcasted_iota(jnp.int32, (CHUNK, CHUNK), 1)
    causal = c_i <= r_i
    off_diag = c_i < (r_i // SUB) * SUB
    lane8 = lax.broadcasted_iota(jnp.int32, (8, CHUNK), 1)
    lane_bits = [(lane8 & (1 << n)) != 0 for n in range(3)]
    mid_block = ((r_i // 8) % 2 == 1) & (c_i // 8 == r_i // 8 - 1)

    def chunk_matmuls(c):
        r0 = c * CHUNK
        sl = pl.ds(r0, CHUNK)
        vc = i_ref[sl, :]
        qt_c = qt_ref[sl, :]

        q_blk, k_blk, b_blk, tot, khat = [], [], [], [], []
        for blk in range(n_sub):
            bs = pl.ds(r0 + blk * SUB, SUB)
            q_blk.append(qf_ref[bs, :])
            k_blk.append(k_ref[bs, :])
            b_blk.append(b_ref[bs, :])
            tot.append(b_ref[pl.ds(r0 + blk * SUB + SUB - 1, 1), :])
            khat.append(k_blk[blk] * jnp.exp2(tot[blk] - b_blk[blk]))

        def span(lo, hi):
            return functools.reduce(lambda a, t: a + t, tot[lo + 1:hi], tot[lo])

        def keys_seen_from(blk):
            parts = [khat[j] * jnp.exp2(span(j + 1, blk)) if j < blk - 1 else khat[j]
                     for j in range(n_sub)]
            return jnp.concatenate(parts, axis=0)

        k_stack = jnp.concatenate([keys_seen_from(blk) for blk in range(1, n_sub)],
                                  axis=0).astype(jnp.bfloat16)
        s_all = lax.dot_general(qt_c.astype(jnp.bfloat16), k_stack, _NT,
                                preferred_element_type=jnp.float32)

        q_mid, k_mid = [], []
        for blk in range(n_sub):
            mid = b_ref[pl.ds(r0 + blk * SUB + 7, 1), :]
            q_mid += [q_blk[blk][0:8, :], q_blk[blk][8:16, :] * jnp.exp2(b_blk[blk][8:16, :] - mid)]
            k_mid += [k_blk[blk][0:8, :] * jnp.exp2(mid - b_blk[blk][0:8, :]), k_blk[blk][8:16, :]]
        s_mid = lax.dot_general(jnp.concatenate(q_mid, axis=0).astype(jnp.bfloat16),
                                jnp.concatenate(k_mid, axis=0).astype(jnp.bfloat16), _NT,
                                preferred_element_type=jnp.float32)

        upd = jnp.dot(vc.astype(jnp.float32).T.astype(jnp.bfloat16),
                      keys_seen_from(n_sub).astype(jnp.bfloat16),
                      preferred_element_type=jnp.float32)

        q_state = jnp.concatenate(
            [qt_c[0:SUB, :]] + [qt_c[blk * SUB:(blk + 1) * SUB, :] * jnp.exp2(span(0, blk))
                                for blk in range(1, n_sub)], axis=0).astype(jnp.bfloat16)
        return dict(r0=r0, sl=sl, vc=vc, q_blk=q_blk, b_blk=b_blk,
                    s_all=s_all, s_mid=s_mid, upd=upd, q_state=q_state,
                    decay=jnp.exp2(span(0, n_sub)))

    def chunk_scores(w):
        r0 = w["r0"]
        off_rows = [jnp.zeros((SUB, CHUNK), jnp.float32)]
        for blk in range(1, n_sub):
            off_rows.append(w["s_all"][blk * SUB:(blk + 1) * SUB, (blk - 1) * CHUNK:blk * CHUNK])
        s_off = jnp.concatenate(off_rows, axis=0)

        diag_rows = []
        for blk in range(n_sub):
            for half in range(SUB // 8):
                q_h = w["q_blk"][blk][half * 8:(half + 1) * 8, :]
                b_h = w["b_blk"][blk][half * 8:(half + 1) * 8, :]
                cols = []
                for s in range(half * 8, (half + 1) * 8):
                    row_s = pl.ds(r0 + blk * SUB + s, 1)
                    d = b_h - b_ref[row_s, :]
                    if s > half * 8:
                        d = jnp.minimum(d, 0.0)
                    cols.append(jnp.sum(q_h * k_ref[row_s, :] * jnp.exp2(d),
                                        axis=-1, keepdims=True))
                bit = 0
                while len(cols) > 1:
                    cols = [jnp.where(lane_bits[bit], cols[n + 1], cols[n])
                            for n in range(0, len(cols), 2)]
                    bit += 1
                diag_rows.append(cols[0])
        s_diag = jnp.concatenate(diag_rows, axis=0)

        return jnp.where(
            causal, jnp.where(off_diag, s_off, jnp.where(mid_block, w["s_mid"], s_diag)),
            0.0).astype(jnp.bfloat16)

    def group_body(g, carry):
        work = [chunk_matmuls(g * GROUP + n) for n in range(GROUP)]
        scores = [chunk_scores(w) for w in work]
        state_t = state_ref[...]
        for w, sc in zip(work, scores):
            o_c = jnp.dot(sc, w["vc"], preferred_element_type=jnp.float32)
            o_c = o_c + lax.dot_general(w["q_state"], state_t.astype(jnp.bfloat16), _NT,
                                        preferred_element_type=jnp.float32)
            oacc_ref[w["sl"], :] = o_c
            state_t = state_t * w["decay"] + w["upd"]
        state_ref[...] = state_t
        return carry

    for g in range(n_chunks // GROUP):
        group_body(g, 0)

    o = oacc_ref[...]
    o = o * _rms_scale(o) * nw_ref[...]
    o_ref[...] = (o * _silu(g_ref[...].astype(jnp.float32))).astype(o_ref.dtype)


def _rope(u, cos_t, sin_t):
    return u * cos_t + pltpu.roll(u, LANE // 2, axis=1) * sin_t


def _rope_table_kernel(pos_ref, invf_ref, cos_ref, sin_ref):
    ang = pos_ref[...].astype(jnp.float32) * invf_ref[...]
    cos_ref[...] = jnp.cos(ang)
    sin_ref[...] = jnp.sin(ang)


def _rope_table(pos_row, invf_col, *, tn):
    m = pos_row.shape[1]
    nf = invf_col.shape[0]
    return pl.pallas_call(
        _rope_table_kernel,
        grid=(m // tn,),
        in_specs=[pl.BlockSpec((1, tn), lambda i: (0, i)),
                  pl.BlockSpec((nf, 1), lambda i: (0, 0))],
        out_specs=[pl.BlockSpec((nf, tn), lambda i: (0, i)),
                   pl.BlockSpec((nf, tn), lambda i: (0, i))],
        out_shape=[jax.ShapeDtypeStruct((nf, m), jnp.float32),
                   jax.ShapeDtypeStruct((nf, m), jnp.float32)],
        compiler_params=pltpu.CompilerParams(dimension_semantics=("parallel",)),
        name="rope_table",
    )(pos_row, invf_col)


def _mla_prep_kernel(cq_ref, ckv_ref, kr_ref, cos_ref, sin_ref, qnw_ref, kvnw_ref, wuq_ref,
                     wukv_ref, qhw_ref, khw_ref, const_ref, q_ref, k_ref, v_ref):
    q_const = const_ref[0:1, :]
    k_const = const_ref[1:2, :]
    cq = cq_ref[...].astype(jnp.float32)
    ckv = ckv_ref[...].astype(jnp.float32)
    cqn = (cq * _rms_scale(cq) * qnw_ref[...]).astype(jnp.bfloat16)
    ckvn = (ckv * _rms_scale(ckv) * kvnw_ref[...]).astype(jnp.bfloat16)

    half = QK_ROPE // 2
    lane = lax.broadcasted_iota(jnp.int32, (1, LANE), 1)
    hi = (lane >= LANE // 2) & (lane < LANE // 2 + half)
    kr_raw = kr_ref[...].astype(jnp.float32)
    kr = jnp.where(lane < half, kr_raw, 0.0) + jnp.where(
        hi, pltpu.roll(kr_raw, LANE // 2 - half, axis=1), 0.0)

    rows = cos_ref.shape[1]
    zpad = jnp.zeros((LANE - half, rows), jnp.float32)
    cos_lo = jnp.concatenate([cos_ref[...], zpad], axis=0).T
    sin_lo = jnp.concatenate([sin_ref[...], zpad], axis=0).T
    cos_t = cos_lo + pltpu.roll(cos_lo, LANE // 2, axis=1)
    sin_t = pltpu.roll(sin_lo, LANE // 2, axis=1) - sin_lo

    qhw = qhw_ref[...]
    khw = khw_ref[...]
    kr_ss = jnp.sum(kr * kr, axis=-1, keepdims=True)
    kr_rot = _rope(kr * khw[:, LANE:QK_PAD], cos_t, sin_t)
    for h in range(MLA_HEADS):
        c0 = h * QK_PAD
        q = jnp.dot(cqn, wuq_ref[:, c0:c0 + QK_PAD], preferred_element_type=jnp.float32)
        qn = q[:, 0:LANE]
        qr = q[:, LANE:QK_PAD]
        ss = jnp.sum(qn * qn + qr * qr, axis=-1, keepdims=True)
        inv = lax.rsqrt(ss * (1.0 / QK_DIM) + NORM_EPS) * (ATTN_SCALE * LOG2_E)
        q_ref[:, c0:c0 + LANE] = (qn * inv * qhw[:, 0:LANE]).astype(q_ref.dtype)
        q_ref[:, c0 + LANE:c0 + QK_PAD] = (_rope(qr * inv * qhw[:, LANE:QK_PAD], cos_t, sin_t)
                                           + q_const).astype(q_ref.dtype)

        kv = jnp.dot(ckvn, wukv_ref[:, c0:c0 + QK_PAD], preferred_element_type=jnp.float32)
        kn = kv[:, 0:LANE]
        ss = jnp.sum(kn * kn, axis=-1, keepdims=True) + kr_ss
        inv = lax.rsqrt(ss * (1.0 / QK_DIM) + NORM_EPS)
        k_ref[:, c0:c0 + LANE] = (kn * inv * khw[:, 0:LANE]).astype(k_ref.dtype)
        k_ref[:, c0 + LANE:c0 + QK_PAD] = (kr_rot * inv + k_const).astype(k_ref.dtype)
        v_ref[:, c0:c0 + V_DIM] = kv[:, LANE:QK_PAD].astype(v_ref.dtype)
        v_ref[:, c0 + V_DIM:c0 + 2 * V_DIM] = jnp.ones((v_ref.shape[0], V_DIM), v_ref.dtype)


def _mla_prep(proj, proj_rope, cos_tab, sin_tab, qnw, kvnw, wuq, wukv, qhw, khw, lane_consts,
              *, tm):
    m = proj.shape[0]
    full = lambda a: pl.BlockSpec(a.shape, lambda i: (0, 0))
    hg_cols = 4 * HG_WIDTH
    nf = cos_tab.shape[0]
    return pl.pallas_call(
        _mla_prep_kernel,
        grid=(m // tm,),
        in_specs=[pl.BlockSpec((tm, Q_LORA), lambda i: (i, hg_cols // Q_LORA)),
                  pl.BlockSpec((tm, KV_LORA), lambda i: (i, (hg_cols + Q_LORA) // KV_LORA)),
                  pl.BlockSpec((tm, LANE), lambda i: (i, 0)),
                  pl.BlockSpec((nf, tm), lambda i: (0, i)),
                  pl.BlockSpec((nf, tm), lambda i: (0, i)),
                  full(qnw), full(kvnw), full(wuq), full(wukv), full(qhw), full(khw),
                  full(lane_consts)],
        out_specs=[pl.BlockSpec((tm, MLA_HEADS * QK_PAD), lambda i: (i, 0)),
                   pl.BlockSpec((tm, MLA_HEADS * QK_PAD), lambda i: (i, 0)),
                   pl.BlockSpec((tm, MLA_HEADS * 2 * V_DIM), lambda i: (i, 0))],
        out_shape=[jax.ShapeDtypeStruct((m, MLA_HEADS * QK_PAD), jnp.bfloat16),
                   jax.ShapeDtypeStruct((m, MLA_HEADS * QK_PAD), jnp.bfloat16),
                   jax.ShapeDtypeStruct((m, MLA_HEADS * 2 * V_DIM), jnp.bfloat16)],
        compiler_params=pltpu.CompilerParams(
            dimension_semantics=("parallel",),
            vmem_limit_bytes=VMEM_LIMIT),
        name="mla_prep",
    )(proj, proj, proj_rope, cos_tab, sin_tab, qnw, kvnw, wuq, wukv, qhw, khw, lane_consts)


def _mixer_kernel(q_ref, k_ref, v_ref, hq_ref, hf_ref, hi_ref, hg_ref, lb_ref, nw_ref,
                  o_ref, oa_ref, sa_ref, sb_ref, m_ref, acc_ref,
                  state_ref, qf_ref, qt_ref, kk_ref, b_ref, oacc_ref,
                  *, tile, n_tiles, hg_rows, scores_bounded):
    n_col = tile // LANE
    half = tile // 2
    t = pl.program_id(2)
    bounded = [False]

    def q_rows(i, start=0, size=tile):
        return pl.ds(pl.multiple_of(i * tile, tile) + start, size)

    def scores(i, j, s_ref):
        ks = pl.ds(pl.multiple_of(j * tile, tile), tile)
        s_ref[...] = lax.dot_general(q_ref[q_rows(i), :], k_ref[ks, :], _NT,
                                     preferred_element_type=jnp.float32)

    def start_tile():
        m_ref[...] = jnp.full_like(m_ref, -jnp.inf)
        acc_ref[...] = jnp.zeros_like(acc_ref)

    def finish_tile(i):
        acc = acc_ref[...]
        o_ref[q_rows(i), :] = (acc[:, :V_DIM] / acc[:, V_DIM:]).astype(o_ref.dtype)

    def diag_scores(i, s_ref):
        for rows, n_keys in ((slice(0, half), half), (slice(half, tile), tile)):
            ks = pl.ds(pl.multiple_of(i * tile, tile), n_keys)
            s_ref[rows, 0:n_keys] = lax.dot_general(
                q_ref[q_rows(i, rows.start, half), :], k_ref[ks, :], _NT,
                preferred_element_type=jnp.float32)

    def hgrn_rows(n):
        view = lambda ref: ref.at[pl.ds(pl.multiple_of(n * hg_rows, hg_rows), hg_rows), :]
        _hgrn_block(view(hq_ref), view(hf_ref), view(hi_ref), view(hg_ref), lb_ref, nw_ref,
                    view(oa_ref), state_ref, qf_ref, qt_ref, kk_ref, b_ref, oacc_ref, rows=hg_rows)

    def softmax_rows(s_ref, rows, n_vis, half_last, n_out):
        cols = [s_ref[rows, c * LANE:(c + 1) * LANE] for c in range(n_vis)]
        if half_last:
            first_chunk = lax.broadcasted_iota(jnp.int32, (1, LANE), 1) < CHUNK
            cols[-1] = jnp.where(first_chunk, cols[-1], -jnp.inf)
        if bounded[0]:
            alpha = None
            p = [jnp.exp2(s_c) for s_c in cols]
        else:
            m_prev = m_ref[rows, :]
            m_cur = functools.reduce(jnp.maximum, cols)
            m_new = jnp.maximum(m_prev, jnp.max(m_cur, axis=-1, keepdims=True))
            alpha = jnp.exp2(m_prev - m_new)
            p = [jnp.exp2(s_c - m_new) for s_c in cols]
            m_ref[rows, :] = m_new
        p += [jnp.zeros_like(p[0])] * (n_out - n_vis)
        return jnp.concatenate(p, axis=1).astype(v_ref.dtype), alpha

    def accumulate(rows, alpha, pv):
        if alpha is None:
            acc_ref[rows, :] = acc_ref[rows, :] + pv
        else:
            acc_ref[rows, :] = jnp.concatenate([alpha, alpha], axis=1) * acc_ref[rows, :] + pv

    def softmax_pv(j, s_ref):
        ks = pl.ds(pl.multiple_of(j * tile, tile), tile)
        p, alpha = softmax_rows(s_ref, slice(None), n_col, False, n_col)
        accumulate(slice(None), alpha,
                   jnp.dot(p, v_ref[ks, :], preferred_element_type=jnp.float32))

    def diag_softmax_pv(i, s_ref):
        lane_chunks = LANE // CHUNK
        for rows, n_keys in ((slice(0, half), half), (slice(half, tile), tile)):
            parts = [softmax_rows(s_ref, slice(c * CHUNK, (c + 1) * CHUNK),
                                  c // lane_chunks + 1, c % lane_chunks != lane_chunks - 1,
                                  n_keys // LANE)
                     for c in range(rows.start // CHUNK, rows.stop // CHUNK)]
            p = jnp.concatenate([pp for pp, _ in parts], axis=0)
            alpha = None if bounded[0] else jnp.concatenate([aa for _, aa in parts], axis=0)
            ks = pl.ds(pl.multiple_of(i * tile, tile), n_keys)
            accumulate(rows, alpha, jnp.dot(p, v_ref[ks, :], preferred_element_type=jnp.float32))

    def query_tile(i, odd, first, other, next_i, hg_first, hg_tail):
        start_tile()

        def pair(jj, carry):
            j = 2 * jj
            scores(i, j + 1, other)
            softmax_pv(j, first)
            scores(i, j + 2, first)
            softmax_pv(j + 1, other)
            hgrn_rows(hg_first + jj)
            return carry

        lax.fori_loop(0, i // 2, pair, 0)
        last, spare = (other, first) if odd else (first, other)
        if odd:
            diag_scores(i, other)
            softmax_pv(i - 1, first)
        if next_i is not None:
            scores(next_i, 0, spare)
        if hg_tail is not None:
            hgrn_rows(hg_tail)
        diag_softmax_pv(i, last)
        finish_tile(i)

    @pl.when(t == 0)
    def _():
        state_ref[...] = jnp.zeros_like(state_ref)

    lo = t
    hi = n_tiles - 1 - t
    hg_last = 2 * tile // hg_rows - 1

    def both_tiles(lo_odd, scores_bounded):
        bounded[0] = scores_bounded
        scores(lo, 0, sa_ref)
        query_tile(lo, lo_odd, sa_ref, sb_ref, hi, 0, None)
        hi_first, hi_other = (sa_ref, sb_ref) if lo_odd else (sb_ref, sa_ref)
        query_tile(hi, not lo_odd, hi_first, hi_other, None, lo // 2, hg_last)

    for lo_odd in (False, True):
        @pl.when(t % 2 == int(lo_odd))
        def _():
            both_tiles(lo_odd, scores_bounded)


def _mixers(q, k, v1, proj, hgrn_lb, norm_w, *, batch, seq, tile, hg_rows, scores_bounded):
    n_tiles = seq // tile
    steps = n_tiles // 2
    rows = seq // steps
    assert n_tiles % 4 == 0 and rows // hg_rows == (n_tiles - 1) // 2 + 1
    vw = 2 * V_DIM
    hb = HG_HEADS

    def hgrn_spec(group):
        return pl.BlockSpec((rows, HG_D), lambda b, h, g: (b * steps + g, group * hb + h))

    scratch = [pltpu.VMEM((tile, tile), jnp.float32),
               pltpu.VMEM((tile, tile), jnp.float32),
               pltpu.VMEM((tile, LANE), jnp.float32),
               pltpu.VMEM((tile, vw), jnp.float32),
               pltpu.VMEM((HG_D, HG_D), jnp.float32)]
    scratch += [pltpu.VMEM((hg_rows, HG_D), jnp.float32) for _ in range(5)]
    out_sds = jax.ShapeDtypeStruct((batch * seq, MLA_HEADS * V_DIM), jnp.bfloat16)
    return pl.pallas_call(
        functools.partial(_mixer_kernel, tile=tile, n_tiles=n_tiles, hg_rows=hg_rows,
                          scores_bounded=scores_bounded),
        grid=(batch, MLA_HEADS, steps),
        in_specs=[pl.BlockSpec((seq, QK_PAD), lambda b, h, g: (b, h)),
                  pl.BlockSpec((seq, QK_PAD), lambda b, h, g: (b, h)),
                  pl.BlockSpec((seq, vw), lambda b, h, g: (b, h)),
                  hgrn_spec(0), hgrn_spec(1), hgrn_spec(2), hgrn_spec(3),
                  pl.BlockSpec((hgrn_lb.shape[0], HG_D), lambda b, h, g: (0, h)),
                  pl.BlockSpec((1, HG_D), lambda b, h, g: (0, 0))],
        out_specs=[pl.BlockSpec((seq, V_DIM), lambda b, h, g: (b, h)),
                   pl.BlockSpec((rows, V_DIM), lambda b, h, g: (b * steps + g, h))],
        out_shape=[out_sds, out_sds],
        scratch_shapes=scratch,
        compiler_params=pltpu.CompilerParams(
            dimension_semantics=("parallel", "parallel", "arbitrary"),
            vmem_limit_bytes=VMEM_LIMIT),
        name="token_mixers_bounded" if scores_bounded else "token_mixers_online",
    )(q, k, v1, proj, proj, proj, proj, hgrn_lb, norm_w)


def _rope_lanes(a):
    half = QK_ROPE // 2
    z = jnp.zeros(a.shape[:-1] + (LANE // 2 - half,), a.dtype)
    return jnp.concatenate([a[..., :half], z, a[..., half:], z], axis=-1)


def _head_lanes(a):
    return jnp.concatenate([a[..., :QK_NOPE], _rope_lanes(a[..., QK_NOPE:])], axis=-1)


def kernel(x, positions, attn_norm_w, w_in, hgrn_lb, hgrn_norm_w, mla_q_norm_w, w_uq,
           mla_kv_norm_w, w_ukv, q_head_norm_w, k_head_norm_w, w_out, ffn_norm_w,
           w_gate_up, w_down):
    batch, seq, d = x.shape
    m = batch * seq
    bf = jnp.bfloat16
    layer = 0
    xf = x.reshape(m, d)

    hg_cols = 4 * HG_WIDTH
    w_in_l = w_in[layer]
    mla_cols = hg_cols + Q_LORA + KV_LORA
    w_rope = jnp.pad(w_in_l[:, mla_cols:], ((0, 0), (0, LANE - QK_ROPE))).astype(bf)
    wuq = _head_lanes(w_uq[layer].reshape(Q_LORA, MLA_HEADS, QK_DIM)).reshape(
        Q_LORA, MLA_HEADS * QK_PAD).astype(bf)
    wukv = w_ukv[layer].astype(bf)
    qhw = _head_lanes(q_head_norm_w[layer])[None, :]
    khw = _head_lanes(k_head_norm_w[layer])[None, :]
    inv_freq = 1.0 / (ROPE_THETA ** (jnp.arange(0, QK_ROPE, 2, dtype=jnp.float32) / QK_ROPE))
    nw_attn = attn_norm_w[layer][None, :]

    score_bound = (QK_DIM * ATTN_SCALE * LOG2_E * SCORE_BOUND_SLACK
                   * jnp.max(jnp.abs(q_head_norm_w[layer])) * jnp.max(jnp.abs(k_head_norm_w[layer]))
                   ).astype(bf).astype(jnp.float32)
    last_lane = (jnp.arange(LANE) == LANE - 1).astype(jnp.float32)
    lane_consts = jnp.stack([last_lane, -score_bound * last_lane])

    proj, proj_rope = _norm_matmul(xf, nw_attn, w_in_l, w_rope, n_main=mla_cols,
                                   tm=1024, tn=1024, out_dtype=bf, name="in_proj")

    cos_tab, sin_tab = _rope_table(positions.reshape(1, m), inv_freq[:, None], tn=2048)
    q, k, v = _mla_prep(proj, proj_rope, cos_tab, sin_tab,
                        mla_q_norm_w[layer][None, :], mla_kv_norm_w[layer][None, :],
                        wuq, wukv, qhw, khw, lane_consts, tm=1024)
    mixers = functools.partial(_mixers, q, k, v, proj, hgrn_lb, hgrn_norm_w[layer][None, :],
                               batch=batch, seq=seq, tile=1024, hg_rows=GROUP * CHUNK)
    o_b, o_a = lax.cond(score_bound <= MAX_FOLDED_BOUND,
                        functools.partial(mixers, scores_bounded=True),
                        functools.partial(mixers, scores_bounded=False))

    x1, h_ffn = _res_matmul([o_a, o_b], w_out[layer].astype(bf), xf, tm=512, tn=d,
                            name="out_proj", norm_w=ffn_norm_w[layer][None, :])

    act = _swiglu(h_ffn, w_gate_up, layer, tm=2048, tn=512)
    x2 = _res_matmul([act], w_down[layer].astype(bf), x1, tm=1024, tn=512, name="ffn_down")
    return x2.reshape(batch, seq, d)
```

```python
import functools

import jax
import jax.numpy as jnp
import numpy as np
from jax import lax
from jax.experimental import pallas as pl
from jax.experimental.pallas import tpu as pltpu

D_MODEL = 2048
CHUNK = 64
SUB = 16
GROUP = 8
HG_WIDTH = 1024
HG_HEADS = 8
HG_D = 128
MLA_HEADS = 8
V_DIM = 128
QK_NOPE = 128
QK_ROPE = 64
QK_DIM = QK_NOPE + QK_ROPE
QK_PAD = 256
Q_LORA = 512
KV_LORA = 512
ROPE_THETA = 10000.0
ATTN_SCALE = QK_DIM ** -0.5
LOG2_E = 1.4426950408889634
SCORE_BOUND_SLACK = 1.04
MAX_FOLDED_BOUND = 50.0
D_FF = 5632
NORM_EPS = 1e-6
LANE = 128

VMEM_LIMIT = 56 * 1024 * 1024

_NT = (((1,), (1,)), ((), ()))


def _silu(x):
    return x * (1.0 / (1.0 + jnp.exp(-x)))


def _rms_scale(x):
    return lax.rsqrt(jnp.mean(x * x, axis=-1, keepdims=True) + NORM_EPS)


def _norm_matmul_kernel(x_ref, nw_ref, w_ref, wt_ref, o_ref, ot_ref, h_ref):
    @pl.when(pl.program_id(1) == 0)
    def _():
        x = x_ref[...]
        h = (x * _rms_scale(x) * nw_ref[...]).astype(h_ref.dtype)
        h_ref[...] = h
        ot_ref[...] = jnp.dot(h, wt_ref[...], preferred_element_type=jnp.float32).astype(ot_ref.dtype)

    o_ref[...] = jnp.dot(h_ref[...], w_ref[...].astype(h_ref.dtype),
                         preferred_element_type=jnp.float32).astype(o_ref.dtype)


def _norm_matmul(x, nw, w, w_tail, *, n_main, tm, tn, out_dtype, name):
    m, k = x.shape
    nt = w_tail.shape[1]
    return pl.pallas_call(
        _norm_matmul_kernel,
        grid=(m // tm, n_main // tn),
        in_specs=[pl.BlockSpec((tm, k), lambda i, j: (i, 0)),
                  pl.BlockSpec((1, k), lambda i, j: (0, 0)),
                  pl.BlockSpec((k, tn), lambda i, j: (0, j)),
                  pl.BlockSpec((k, nt), lambda i, j: (0, 0))],
        out_specs=[pl.BlockSpec((tm, tn), lambda i, j: (i, j)),
                   pl.BlockSpec((tm, nt), lambda i, j: (i, 0))],
        out_shape=[jax.ShapeDtypeStruct((m, n_main), out_dtype),
                   jax.ShapeDtypeStruct((m, nt), out_dtype)],
        scratch_shapes=[pltpu.VMEM((tm, k), jnp.bfloat16)],
        compiler_params=pltpu.CompilerParams(
            dimension_semantics=("parallel", "arbitrary"),
            vmem_limit_bytes=VMEM_LIMIT),
        name=name,
    )(x, nw, w, w_tail)


def _swiglu_kernel(h_ref, wg_ref, wu_ref, o_ref):
    h = h_ref[...]
    g = jnp.dot(h, wg_ref[...].astype(h.dtype), preferred_element_type=jnp.float32)
    u = jnp.dot(h, wu_ref[...].astype(h.dtype), preferred_element_type=jnp.float32)
    o_ref[...] = (_silu(g) * u).astype(o_ref.dtype)


def _swiglu(h, w_gate_up, layer, *, tm, tn):
    m, k = h.shape
    n = w_gate_up.shape[2] // 2
    up_off = n // tn
    return pl.pallas_call(
        _swiglu_kernel,
        grid=(m // tm, n // tn),
        in_specs=[pl.BlockSpec((tm, k), lambda i, j: (i, 0)),
                  pl.BlockSpec((None, k, tn), lambda i, j: (layer, 0, j)),
                  pl.BlockSpec((None, k, tn), lambda i, j: (layer, 0, j + up_off))],
        out_specs=pl.BlockSpec((tm, tn), lambda i, j: (i, j)),
        out_shape=jax.ShapeDtypeStruct((m, n), jnp.bfloat16),
        compiler_params=pltpu.CompilerParams(
            dimension_semantics=("parallel", "arbitrary"),
            vmem_limit_bytes=VMEM_LIMIT),
        name="ffn_up",
    )(h, w_gate_up, w_gate_up)


def _res_matmul_kernel(*refs, n_a, with_norm):
    a_refs = refs[:n_a]
    w_refs = refs[n_a:2 * n_a]
    r_ref = refs[2 * n_a]
    acc = r_ref[...]
    for a_ref, w_ref in zip(a_refs, w_refs):
        acc = acc + jnp.dot(a_ref[...], w_ref[...], preferred_element_type=jnp.float32)
    if with_norm:
        nw_ref, o_ref, h_ref = refs[2 * n_a + 1:]
        h_ref[...] = (acc * _rms_scale(acc) * nw_ref[...]).astype(h_ref.dtype)
    else:
        o_ref = refs[2 * n_a + 1]
    o_ref[...] = acc


def _res_matmul(a_list, w, res, *, tm, tn, name, norm_w=None):
    m, n = res.shape
    n_a = len(a_list)
    ka = a_list[0].shape[1]
    with_norm = norm_w is not None
    assert not with_norm or tn == n
    in_specs = [pl.BlockSpec((tm, ka), lambda i, j: (i, 0)) for _ in a_list]
    in_specs += [pl.BlockSpec((ka, tn), functools.partial(lambda i, j, kb: (kb, j), kb=kb))
                 for kb in range(n_a)]
    in_specs += [pl.BlockSpec((tm, tn), lambda i, j: (i, j))]
    out_specs = [pl.BlockSpec((tm, tn), lambda i, j: (i, j))]
    out_shape = [jax.ShapeDtypeStruct((m, n), jnp.float32)]
    operands = [*a_list, *([w] * n_a), res]
    if with_norm:
        in_specs += [pl.BlockSpec((1, n), lambda i, j: (0, 0))]
        out_specs += [pl.BlockSpec((tm, tn), lambda i, j: (i, j))]
        out_shape += [jax.ShapeDtypeStruct((m, n), jnp.bfloat16)]
        operands += [norm_w]
    out = pl.pallas_call(
        functools.partial(_res_matmul_kernel, n_a=n_a, with_norm=with_norm),
        grid=(m // tm, n // tn),
        in_specs=in_specs,
        out_specs=out_specs,
        out_shape=out_shape,
        compiler_params=pltpu.CompilerParams(
            dimension_semantics=("parallel", "arbitrary"),
            vmem_limit_bytes=VMEM_LIMIT),
        name=name,
    )(*operands)
    return out if with_norm else out[0]


def _segmented_cumsum(x, row, seg):
    pos = row % seg
    shift = 1
    while shift < seg:
        x = x + jnp.where(pos >= shift, pltpu.roll(x, shift, axis=0), 0.0)
        shift *= 2
    return x


def _hgrn_block(q_ref, f_ref, i_ref, g_ref, lb_ref, nw_ref, o_ref,
                state_ref, qf_ref, qt_ref, k_ref, b_ref, oacc_ref, *, rows):
    n_chunks = rows // CHUNK
    n_sub = CHUNK // SUB

    lbr = lb_ref[...]
    lbe = jnp.exp(lbr - jnp.max(lbr, axis=0, keepdims=True))
    lb = lbe[0:1, :] / jnp.sum(lbe, axis=0, keepdims=True)

    qf = _silu(q_ref[...].astype(jnp.float32))
    f = lb + (1.0 - lb) * (1.0 / (1.0 + jnp.exp(-f_ref[...].astype(jnp.float32))))
    row = lax.broadcasted_iota(jnp.int32, (rows, HG_D), 0)
    b = _segmented_cumsum(jnp.log2(f), row, SUB)
    qf_ref[...] = qf
    qt_ref[...] = qf * jnp.exp2(b)
    k_ref[...] = 1.0 - f
    b_ref[...] = b

    r_i = lax.broadcasted_iota(jnp.int32, (CHUNK, CHUNK), 0)
    c_i = lax.broadcasted_iota(jnp.int32, (CHUNK, CHUNK), 1)
    causal = c_i <= r_i
    off_diag = c_i < (r_i // SUB) * SUB
    lane8 = lax.broadcasted_iota(jnp.int32, (8, CHUNK), 1)
    lane_bits = [(lane8 & (1 << n)) != 0 for n in range(3)]
    mid_block = ((r_i // 8) % 2 == 1) & (c_i // 8 == r_i // 8 - 1)

    def chunk_matmuls(c):
        r0 = c * CHUNK
        sl = pl.ds(r0, CHUNK)
        vc = i_ref[sl, :]
        qt_c = qt_ref[sl, :]

        q_blk, k_blk, b_blk, tot, khat = [], [], [], [], []
        for blk in range(n_sub):
            bs = pl.ds(r0 + blk * SUB, SUB)
            q_blk.append(qf_ref[bs, :])
            k_blk.append(k_ref[bs, :])
            b_blk.append(b_ref[bs, :])
            tot.append(b_ref[pl.ds(r0 + blk * SUB + SUB - 1, 1), :])
            khat.append(k_blk[blk] * jnp.exp2(tot[blk] - b_blk[blk]))

        def span(lo, hi):
            return functools.reduce(lambda a, t: a + t, tot[lo + 1:hi], tot[lo])

        def keys_seen_from(blk):
            parts = [khat[j] * jnp.exp2(span(j + 1, blk)) if j < blk - 1 else khat[j]
                     for j in range(n_sub)]
            return jnp.concatenate(parts, axis=0)

        k_stack = jnp.concatenate([keys_seen_from(blk) for blk in range(1, n_sub)],
                                  axis=0).astype(jnp.bfloat16)
        s_all = lax.dot_general(qt_c.astype(jnp.bfloat16), k_stack, _NT,
                                preferred_element_type=jnp.float32)

        q_mid, k_mid = [], []
        for blk in range(n_sub):
            mid = b_ref[pl.ds(r0 + blk * SUB + 7, 1), :]
            q_mid += [q_blk[blk][0:8, :], q_blk[blk][8:16, :] * jnp.exp2(b_blk[blk][8:16, :] - mid)]
            k_mid += [k_blk[blk][0:8, :] * jnp.exp2(mid - b_blk[blk][0:8, :]), k_blk[blk][8:16, :]]
        s_mid = lax.dot_general(jnp.concatenate(q_mid, axis=0).astype(jnp.bfloat16),
                                jnp.concatenate(k_mid, axis=0).astype(jnp.bfloat16), _NT,
                                preferred_element_type=jnp.float32)

        upd = jnp.dot(vc.astype(jnp.float32).T.astype(jnp.bfloat16),
                      keys_seen_from(n_sub).astype(jnp.bfloat16),
                      preferred_element_type=jnp.float32)

        q_state = jnp.concatenate(
            [qt_c[0:SUB, :]] + [qt_c[blk * SUB:(blk + 1) * SUB, :] * jnp.exp2(span(0, blk))
                                for blk in range(1, n_sub)], axis=0).astype(jnp.bfloat16)
        return dict(r0=r0, sl=sl, vc=vc, q_blk=q_blk, b_blk=b_blk,
                    s_all=s_all, s_mid=s_mid, upd=upd, q_state=q_state,
                    decay=jnp.exp2(span(0, n_sub)))

    def chunk_scores(w):
        r0 = w["r0"]
        off_rows = [jnp.zeros((SUB, CHUNK), jnp.float32)]
        for blk in range(1, n_sub):
            off_rows.append(w["s_all"][blk * SUB:(blk + 1) * SUB, (blk - 1) * CHUNK:blk * CHUNK])
        s_off = jnp.concatenate(off_rows, axis=0)

        diag_rows = []
        for blk in range(n_sub):
            for half in range(SUB // 8):
                q_h = w["q_blk"][blk][half * 8:(half + 1) * 8, :]
                b_h = w["b_blk"][blk][half * 8:(half + 1) * 8, :]
                cols = []
                for s in range(half * 8, (half + 1) * 8):
                    row_s = pl.ds(r0 + blk * SUB + s, 1)
                    d = b_h - b_ref[row_s, :]
                    if s > half * 8:
                        d = jnp.minimum(d, 0.0)
                    cols.append(jnp.sum(q_h * k_ref[row_s, :] * jnp.exp2(d),
                                        axis=-1, keepdims=True))
                bit = 0
                while len(cols) > 1:
                    cols = [jnp.where(lane_bits[bit], cols[n + 1], cols[n])
                            for n in range(0, len(cols), 2)]
                    bit += 1
                diag_rows.append(cols[0])
        s_diag = jnp.concatenate(diag_rows, axis=0)

        return jnp.where(
            causal, jnp.where(off_diag, s_off, jnp.where(mid_block, w["s_mid"], s_diag)),
            0.0).astype(jnp.bfloat16)

    def group_body(g, carry):
        work = [chunk_matmuls(g * GROUP + n) for n in range(GROUP)]
        scores = [chunk_scores(w) for w in work]
        state_t = state_ref[...]
        for w, sc in zip(work, scores):
            o_c = jnp.dot(sc, w["vc"], preferred_element_type=jnp.float32)
            o_c = o_c + lax.dot_general(w["q_state"], state_t.astype(jnp.bfloat16), _NT,
                                        preferred_element_type=jnp.float32)
            oacc_ref[w["sl"], :] = o_c
            state_t = state_t * w["decay"] + w["upd"]
        state_ref[...] = state_t
        return carry

    for g in range(n_chunks // GROUP):
        group_body(g, 0)

    o = oacc_ref[...]
    o = o * _rms_scale(o) * nw_ref[...]
    o_ref[...] = (o * _silu(g_ref[...].astype(jnp.float32))).astype(o_ref.dtype)


def _rope(u, cos_t, sin_t):
    return u * cos_t + pltpu.roll(u, LANE // 2, axis=1) * sin_t


def _rope_table_kernel(pos_ref, invf_ref, cos_ref, sin_ref):
    ang = pos_ref[...].astype(jnp.float32) * invf_ref[...]
    cos_ref[...] = jnp.cos(ang)
    sin_ref[...] = jnp.sin(ang)


def _rope_table(pos_row, invf_col, *, tn):
    m = pos_row.shape[1]
    nf = invf_col.shape[0]
    return pl.pallas_call(
        _rope_table_kernel,
        grid=(m // tn,),
        in_specs=[pl.BlockSpec((1, tn), lambda i: (0, i)),
                  pl.BlockSpec((nf, 1), lambda i: (0, 0))],
        out_specs=[pl.BlockSpec((nf, tn), lambda i: (0, i)),
                   pl.BlockSpec((nf, tn), lambda i: (0, i))],
        out_shape=[jax.ShapeDtypeStruct((nf, m), jnp.float32),
                   jax.ShapeDtypeStruct((nf, m), jnp.float32)],
        compiler_params=pltpu.CompilerParams(dimension_semantics=("parallel",)),
        name="rope_table",
    )(pos_row, invf_col)


def _mla_prep_kernel(cq_ref, ckv_ref, kr_ref, cos_ref, sin_ref, qnw_ref, kvnw_ref, wuq_ref,
                     wukv_ref, qhw_ref, khw_ref, const_ref, q_ref, k_ref, v_ref):
    q_const = const_ref[0:1, :]
    k_const = const_ref[1:2, :]
    cq = cq_ref[...].astype(jnp.float32)
    ckv = ckv_ref[...].astype(jnp.float32)
    cqn = (cq * _rms_scale(cq) * qnw_ref[...]).astype(jnp.bfloat16)
    ckvn = (ckv * _rms_scale(ckv) * kvnw_ref[...]).astype(jnp.bfloat16)

    half = QK_ROPE // 2
    lane = lax.broadcasted_iota(jnp.int32, (1, LANE), 1)
    hi = (lane >= LANE // 2) & (lane < LANE // 2 + half)
    kr_raw = kr_ref[...].astype(jnp.float32)
    kr = jnp.where(lane < half, kr_raw, 0.0) + jnp.where(
        hi, pltpu.roll(kr_raw, LANE // 2 - half, axis=1), 0.0)

    rows = cos_ref.shape[1]
    zpad = jnp.zeros((LANE - half, rows), jnp.float32)
    cos_lo = jnp.concatenate([cos_ref[...], zpad], axis=0).T
    sin_lo = jnp.concatenate([sin_ref[...], zpad], axis=0).T
    cos_t = cos_lo + pltpu.roll(cos_lo, LANE // 2, axis=1)
    sin_t = pltpu.roll(sin_lo, LANE // 2, axis=1) - sin_lo

    qhw = qhw_ref[...]
    khw = khw_ref[...]
    kr_ss = jnp.sum(kr * kr, axis=-1, keepdims=True)
    kr_rot = _rope(kr * khw[:, LANE:QK_PAD], cos_t, sin_t)
    for h in range(MLA_HEADS):
        c0 = h * QK_PAD
        q = jnp.dot(cqn, wuq_ref[:, c0:c0 + QK_PAD], preferred_element_type=jnp.float32)
        qn = q[:, 0:LANE]
        qr = q[:, LANE:QK_PAD]
        ss = jnp.sum(qn * qn + qr * qr, axis=-1, keepdims=True)
        inv = lax.rsqrt(ss * (1.0 / QK_DIM) + NORM_EPS) * (ATTN_SCALE * LOG2_E)
        q_ref[:, c0:c0 + LANE] = (qn * inv * qhw[:, 0:LANE]).astype(q_ref.dtype)
        q_ref[:, c0 + LANE:c0 + QK_PAD] = (_rope(qr * inv * qhw[:, LANE:QK_PAD], cos_t, sin_t)
                                           + q_const).astype(q_ref.dtype)

        kv = jnp.dot(ckvn, wukv_ref[:, c0:c0 + QK_PAD], preferred_element_type=jnp.float32)
        kn = kv[:, 0:LANE]
        ss = jnp.sum(kn * kn, axis=-1, keepdims=True) + kr_ss
        inv = lax.rsqrt(ss * (1.0 / QK_DIM) + NORM_EPS)
        k_ref[:, c0:c0 + LANE] = (kn * inv * khw[:, 0:LANE]).astype(k_ref.dtype)
        k_ref[:, c0 + LANE:c0 + QK_PAD] = (kr_rot * inv + k_const).astype(k_ref.dtype)
        v_ref[:, c0:c0 + V_DIM] = kv[:, LANE:QK_PAD].astype(v_ref.dtype)
        v_ref[:, c0 + V_DIM:c0 + 2 * V_DIM] = jnp.ones((v_ref.shape[0], V_DIM), v_ref.dtype)


def _mla_prep(proj, proj_rope, cos_tab, sin_tab, qnw, kvnw, wuq, wukv, qhw, khw, lane_consts,
              *, tm):
    m = proj.shape[0]
    full = lambda a: pl.BlockSpec(a.shape, lambda i: (0, 0))
    hg_cols = 4 * HG_WIDTH
    nf = cos_tab.shape[0]
    return pl.pallas_call(
        _mla_prep_kernel,
        grid=(m // tm,),
        in_specs=[pl.BlockSpec((tm, Q_LORA), lambda i: (i, hg_cols // Q_LORA)),
                  pl.BlockSpec((tm, KV_LORA), lambda i: (i, (hg_cols + Q_LORA) // KV_LORA)),
                  pl.BlockSpec((tm, LANE), lambda i: (i, 0)),
                  pl.BlockSpec((nf, tm), lambda i: (0, i)),
                  pl.BlockSpec((nf, tm), lambda i: (0, i)),
                  full(qnw), full(kvnw), full(wuq), full(wukv), full(qhw), full(khw),
                  full(lane_consts)],
        out_specs=[pl.BlockSpec((tm, MLA_HEADS * QK_PAD), lambda i: (i, 0)),
                   pl.BlockSpec((tm, MLA_HEADS * QK_PAD), lambda i: (i, 0)),
                   pl.BlockSpec((tm, MLA_HEADS * 2 * V_DIM), lambda i: (i, 0))],
        out_shape=[jax.ShapeDtypeStruct((m, MLA_HEADS * QK_PAD), jnp.bfloat16),
                   jax.ShapeDtypeStruct((m, MLA_HEADS * QK_PAD), jnp.bfloat16),
                   jax.ShapeDtypeStruct((m, MLA_HEADS * 2 * V_DIM), jnp.bfloat16)],
        compiler_params=pltpu.CompilerParams(
            dimension_semantics=("parallel",),
            vmem_limit_bytes=VMEM_LIMIT),
        name="mla_prep",
    )(proj, proj, proj_rope, cos_tab, sin_tab, qnw, kvnw, wuq, wukv, qhw, khw, lane_consts)


def _mixer_kernel(q_ref, k_ref, v_ref, hq_ref, hf_ref, hi_ref, hg_ref, lb_ref, nw_ref,
                  o_ref, oa_ref, sa_ref, sb_ref, m_ref, acc_ref,
                  state_ref, qf_ref, qt_ref, kk_ref, b_ref, oacc_ref,
                  *, tile, n_tiles, hg_rows, scores_bounded):
    n_col = tile // LANE
    half = tile // 2
    t = pl.program_id(2)
    bounded = [False]

    def q_rows(i, start=0, size=tile):
        return pl.ds(pl.multiple_of(i * tile, tile) + start, size)

    def scores(i, j, s_ref):
        ks = pl.ds(pl.multiple_of(j * tile, tile), tile)
        s_ref[...] = lax.dot_general(q_ref[q_rows(i), :], k_ref[ks, :], _NT,
                                     preferred_element_type=jnp.float32)

    def start_tile():
        m_ref[...] = jnp.full_like(m_ref, -jnp.inf)
        acc_ref[...] = jnp.zeros_like(acc_ref)

    def finish_tile(i):
        acc = acc_ref[...]
        o_ref[q_rows(i), :] = (acc[:, :V_DIM] / acc[:, V_DIM:]).astype(o_ref.dtype)

    def diag_scores(i, s_ref):
        for rows, n_keys in ((slice(0, half), half), (slice(half, tile), tile)):
            ks = pl.ds(pl.multiple_of(i * tile, tile), n_keys)
            s_ref[rows, 0:n_keys] = lax.dot_general(
                q_ref[q_rows(i, rows.start, half), :], k_ref[ks, :], _NT,
                preferred_element_type=jnp.float32)

    def hgrn_rows(n):
        view = lambda ref: ref.at[pl.ds(pl.multiple_of(n * hg_rows, hg_rows), hg_rows), :]
        _hgrn_block(view(hq_ref), view(hf_ref), view(hi_ref), view(hg_ref), lb_ref, nw_ref,
                    view(oa_ref), state_ref, qf_ref, qt_ref, kk_ref, b_ref, oacc_ref, rows=hg_rows)

    def softmax_rows(s_ref, rows, n_vis, half_last, n_out):
        cols = [s_ref[rows, c * LANE:(c + 1) * LANE] for c in range(n_vis)]
        if half_last:
            first_chunk = lax.broadcasted_iota(jnp.int32, (1, LANE), 1) < CHUNK
            cols[-1] = jnp.where(first_chunk, cols[-1], -jnp.inf)
        if bounded[0]:
            alpha = None
            p = [jnp.exp2(s_c) for s_c in cols]
        else:
            m_prev = m_ref[rows, :]
            m_cur = functools.reduce(jnp.maximum, cols)
            m_new = jnp.maximum(m_prev, jnp.max(m_cur, axis=-1, keepdims=True))
            alpha = jnp.exp2(m_prev - m_new)
            p = [jnp.exp2(s_c - m_new) for s_c in cols]
            m_ref[rows, :] = m_new
        p += [jnp.zeros_like(p[0])] * (n_out - n_vis)
        return jnp.concatenate(p, axis=1).astype(v_ref.dtype), alpha

    def accumulate(rows, alpha, pv):
        if alpha is None:
            acc_ref[rows, :] = acc_ref[rows, :] + pv
        else:
            acc_ref[rows, :] = jnp.concatenate([alpha, alpha], axis=1) * acc_ref[rows, :] + pv

    def softmax_pv(j, s_ref):
        ks = pl.ds(pl.multiple_of(j * tile, tile), tile)
        p, alpha = softmax_rows(s_ref, slice(None), n_col, False, n_col)
        accumulate(slice(None), alpha,
                   jnp.dot(p, v_ref[ks, :], preferred_element_type=jnp.float32))

    def diag_softmax_pv(i, s_ref):
        lane_chunks = LANE // CHUNK
        for rows, n_keys in ((slice(0, half), half), (slice(half, tile), tile)):
            parts = [softmax_rows(s_ref, slice(c * CHUNK, (c + 1) * CHUNK),
                                  c // lane_chunks + 1, c % lane_chunks != lane_chunks - 1,
                                  n_keys // LANE)
                     for c in range(rows.start // CHUNK, rows.stop // CHUNK)]
            p = jnp.concatenate([pp for pp, _ in parts], axis=0)
            alpha = None if bounded[0] else jnp.concatenate([aa for _, aa in parts], axis=0)
            ks = pl.ds(pl.multiple_of(i * tile, tile), n_keys)
            accumulate(rows, alpha, jnp.dot(p, v_ref[ks, :], preferred_element_type=jnp.float32))

    def query_tile(i, odd, first, other, next_i, hg_first, hg_tail):
        start_tile()

        def pair(jj, carry):
            j = 2 * jj
            scores(i, j + 1, other)
            softmax_pv(j, first)
            hgrn_rows(hg_first + jj)
            scores(i, j + 2, first)
            softmax_pv(j + 1, other)
            return carry

        lax.fori_loop(0, i // 2, pair, 0)
        last, spare = (other, first) if odd else (first, other)
        if odd:
            diag_scores(i, other)
            softmax_pv(i - 1, first)
        if next_i is not None:
            scores(next_i, 0, spare)
        if hg_tail is not None:
            hgrn_rows(hg_tail)
        diag_softmax_pv(i, last)
        finish_tile(i)

    @pl.when(t == 0)
    def _():
        state_ref[...] = jnp.zeros_like(state_ref)

    lo = t
    hi = n_tiles - 1 - t
    hg_last = 2 * tile // hg_rows - 1

    def both_tiles(lo_odd, scores_bounded):
        bounded[0] = scores_bounded
        scores(lo, 0, sa_ref)
        query_tile(lo, lo_odd, sa_ref, sb_ref, hi, 0, None)
        hi_first, hi_other = (sa_ref, sb_ref) if lo_odd else (sb_ref, sa_ref)
        query_tile(hi, not lo_odd, hi_first, hi_other, None, lo // 2, hg_last)

    for lo_odd in (False, True):
        @pl.when(t % 2 == int(lo_odd))
        def _():
            both_tiles(lo_odd, scores_bounded)


def _mixers(q, k, v1, proj, hgrn_lb, norm_w, *, batch, seq, tile, hg_rows, scores_bounded):
    n_tiles = seq // tile
    steps = n_tiles // 2
    rows = seq // steps
    assert n_tiles % 4 == 0 and rows // hg_rows == (n_tiles - 1) // 2 + 1
    vw = 2 * V_DIM
    hb = HG_HEADS

    def hgrn_spec(group):
        return pl.BlockSpec((rows, HG_D), lambda b, h, g: (b * steps + g, group * hb + h))

    scratch = [pltpu.VMEM((tile, tile), jnp.float32),
               pltpu.VMEM((tile, tile), jnp.float32),
               pltpu.VMEM((tile, LANE), jnp.float32),
               pltpu.VMEM((tile, vw), jnp.float32),
               pltpu.VMEM((HG_D, HG_D), jnp.float32)]
    scratch += [pltpu.VMEM((hg_rows, HG_D), jnp.float32) for _ in range(5)]
    out_sds = jax.ShapeDtypeStruct((batch * seq, MLA_HEADS * V_DIM), jnp.bfloat16)
    return pl.pallas_call(
        functools.partial(_mixer_kernel, tile=tile, n_tiles=n_tiles, hg_rows=hg_rows,
                          scores_bounded=scores_bounded),
        grid=(batch, MLA_HEADS, steps),
        in_specs=[pl.BlockSpec((seq, QK_PAD), lambda b, h, g: (b, h)),
                  pl.BlockSpec((seq, QK_PAD), lambda b, h, g: (b, h)),
                  pl.BlockSpec((seq, vw), lambda b, h, g: (b, h)),
                  hgrn_spec(0), hgrn_spec(1), hgrn_spec(2), hgrn_spec(3),
                  pl.BlockSpec((hgrn_lb.shape[0], HG_D), lambda b, h, g: (0, h)),
                  pl.BlockSpec((1, HG_D), lambda b, h, g: (0, 0))],
        out_specs=[pl.BlockSpec((seq, V_DIM), lambda b, h, g: (b, h)),
                   pl.BlockSpec((rows, V_DIM), lambda b, h, g: (b * steps + g, h))],
        out_shape=[out_sds, out_sds],
        scratch_shapes=scratch,
        compiler_params=pltpu.CompilerParams(
            dimension_semantics=("parallel", "parallel", "arbitrary"),
            vmem_limit_bytes=VMEM_LIMIT),
        name="token_mixers_bounded" if scores_bounded else "token_mixers_online",
    )(q, k, v1, proj, proj, proj, proj, hgrn_lb, norm_w)


def _rope_lanes(a):
    half = QK_ROPE // 2
    z = jnp.zeros(a.shape[:-1] + (LANE // 2 - half,), a.dtype)
    return jnp.concatenate([a[..., :half], z, a[..., half:], z], axis=-1)


def _head_lanes(a):
    return jnp.concatenate([a[..., :QK_NOPE], _rope_lanes(a[..., QK_NOPE:])], axis=-1)


def kernel(x, positions, attn_norm_w, w_in, hgrn_lb, hgrn_norm_w, mla_q_norm_w, w_uq,
           mla_kv_norm_w, w_ukv, q_head_norm_w, k_head_norm_w, w_out, ffn_norm_w,
           w_gate_up, w_down):
    batch, seq, d = x.shape
    m = batch * seq
    bf = jnp.bfloat16
    layer = 0
    xf = x.reshape(m, d)

    hg_cols = 4 * HG_WIDTH
    w_in_l = w_in[layer]
    mla_cols = hg_cols + Q_LORA + KV_LORA
    w_rope = jnp.pad(w_in_l[:, mla_cols:], ((0, 0), (0, LANE - QK_ROPE))).astype(bf)
    wuq = _head_lanes(w_uq[layer].reshape(Q_LORA, MLA_HEADS, QK_DIM)).reshape(
        Q_LORA, MLA_HEADS * QK_PAD).astype(bf)
    wukv = w_ukv[layer].astype(bf)
    qhw = _head_lanes(q_head_norm_w[layer])[None, :]
    khw = _head_lanes(k_head_norm_w[layer])[None, :]
    inv_freq = 1.0 / (ROPE_THETA ** (jnp.arange(0, QK_ROPE, 2, dtype=jnp.float32) / QK_ROPE))
    nw_attn = attn_norm_w[layer][None, :]

    score_bound = (QK_DIM * ATTN_SCALE * LOG2_E * SCORE_BOUND_SLACK
                   * jnp.max(jnp.abs(q_head_norm_w[layer])) * jnp.max(jnp.abs(k_head_norm_w[layer]))
                   ).astype(bf).astype(jnp.float32)
    last_lane = (jnp.arange(LANE) == LANE - 1).astype(jnp.float32)
    lane_consts = jnp.stack([last_lane, -score_bound * last_lane])

    proj, proj_rope = _norm_matmul(xf, nw_attn, w_in_l, w_rope, n_main=mla_cols,
                                   tm=1024, tn=1024, out_dtype=bf, name="in_proj")

    cos_tab, sin_tab = _rope_table(positions.reshape(1, m), inv_freq[:, None], tn=2048)
    q, k, v = _mla_prep(proj, proj_rope, cos_tab, sin_tab,
                        mla_q_norm_w[layer][None, :], mla_kv_norm_w[layer][None, :],
                        wuq, wukv, qhw, khw, lane_consts, tm=1024)
    mixers = functools.partial(_mixers, q, k, v, proj, hgrn_lb, hgrn_norm_w[layer][None, :],
                               batch=batch, seq=seq, tile=1024, hg_rows=GROUP * CHUNK)
    o_b, o_a = lax.cond(score_bound <= MAX_FOLDED_BOUND,
                        functools.partial(mixers, scores_bounded=True),
                        functools.partial(mixers, scores_bounded=False))

    x1, h_ffn = _res_matmul([o_a, o_b], w_out[layer].astype(bf), xf, tm=512, tn=d,
                            name="out_proj", norm_w=ffn_norm_w[layer][None, :])

    act = _swiglu(h_ffn, w_gate_up, layer, tm=2048, tn=512)
    x2 = _res_matmul([act], w_down[layer].astype(bf), x1, tm=1024, tn=512, name="ffn_down")
    return x2.reshape(batch, seq, d)
```
